```python
import math
import jax
import jax.numpy as jnp
from jax import lax
import numpy as np

D_MODEL = 2048
BATCH = 4
SEQ = 8192
DEPTH = 4

GRID_W = 64
CTX_LEN = 256
EPS = 1e-6

NA_HEAD_DIM = 128
NA_HEADS = D_MODEL // NA_HEAD_DIM
NA_WIDTH = NA_HEADS * NA_HEAD_DIM
WIN_H = 8
WIN_W = 16
ROPE_THETA = 10000.0

SSM_WIDTH = D_MODEL
SSM_HEAD_DIM = 64
SSM_HEADS = SSM_WIDTH // SSM_HEAD_DIM
SSM_STATE = 128
SSM_GROUPS = 8
D_CONV = 5
CHUNK = 128
CONV_CH = SSM_WIDTH + 2 * SSM_GROUPS * SSM_STATE

MIX_WIDTH = NA_WIDTH + SSM_WIDTH
IN_COLS = 4 * NA_WIDTH + SSM_WIDTH + CONV_CH + 2 * SSM_HEADS

kernel_name = "hybrid_na_ssd_prefix_dit"


def rms_norm(u, w):
    uf = u.astype(jnp.float32)
    y = uf * lax.rsqrt(jnp.mean(uf * uf, axis=-1, keepdims=True) + EPS)
    return (y * w.astype(jnp.float32)).astype(u.dtype)


def modulation(cond, ada_w, ada_b):
    m = jax.nn.silu(cond) @ ada_w + ada_b
    return jnp.split(m, 3, axis=-1)


def axial_rope_tables(n_tokens, head_dim):
    t = jnp.arange(n_tokens)
    rows = (t // GRID_W).astype(jnp.float32)
    cols = (t % GRID_W).astype(jnp.float32)
    n_pairs = head_dim // 4
    freqs = ROPE_THETA ** (-jnp.arange(n_pairs, dtype=jnp.float32) / n_pairs)
    ang = jnp.concatenate([rows[:, None] * freqs, cols[:, None] * freqs], axis=-1)
    return jnp.cos(ang), jnp.sin(ang)


def apply_rope(u, cos, sin):
    uf = u.astype(jnp.float32).reshape(*u.shape[:-1], -1, 2)
    u1, u2 = uf[..., 0], uf[..., 1]
    cs, sn = cos[None, :, None, :], sin[None, :, None, :]
    out = jnp.stack([u1 * cs - u2 * sn, u1 * sn + u2 * cs], axis=-1)
    return out.reshape(u.shape).astype(u.dtype)


def context_attention(q, k, v):
    scale = q.shape[-1] ** -0.5
    s = jnp.einsum("blhd,bmhd->bhlm", q, k).astype(jnp.float32) * scale
    p = jax.nn.softmax(s, axis=-1).astype(v.dtype)
    o = jnp.einsum("bhlm,bmhd->blhd", p, v)
    return o.reshape(*o.shape[:2], -1)


def neighbourhood_attention(q, k, v, k_ctx, v_ctx, rpb):
    bsz, n_tok, n_heads, hd = q.shape
    rows = n_tok // GRID_W
    kh, kw = min(WIN_H, rows), WIN_W
    scale = hd ** -0.5
    qg = q.reshape(bsz, rows, GRID_W, n_heads, hd).transpose(1, 0, 3, 2, 4)
    kg = k.reshape(bsz, rows, GRID_W, n_heads, hd).transpose(0, 3, 1, 2, 4)
    vg = v.reshape(bsz, rows, GRID_W, n_heads, hd).transpose(0, 3, 1, 2, 4)
    col = jnp.arange(GRID_W)
    c0 = jnp.clip(col - kw // 2, 0, GRID_W - kw)
    col_mask = (col[None, :] >= c0[:, None]) & (col[None, :] < c0[:, None] + kw)
    col_idx = jnp.clip(col[None, :] - col[:, None] + WIN_W - 1, 0, 2 * WIN_W - 2)
    rpb_cols = rpb[:, :, col_idx]
    n_loc = kh * GRID_W

    def row_block(args):
        r, q_r = args
        r0 = jnp.clip(r - kh // 2, 0, rows - kh)
        k_r = lax.dynamic_slice_in_dim(kg, r0, kh, axis=2).reshape(bsz, n_heads, n_loc, hd)
        v_r = lax.dynamic_slice_in_dim(vg, r0, kh, axis=2).reshape(bsz, n_heads, n_loc, hd)
        row_idx = r0 + jnp.arange(kh) - r + WIN_H - 1
        bias = rpb_cols[:, row_idx].transpose(0, 2, 1, 3)
        s_loc = jnp.einsum("bhqd,bhkd->bhqk", q_r, k_r).astype(jnp.float32) * scale
        s_loc = s_loc.reshape(bsz, n_heads, GRID_W, kh, GRID_W) + bias.astype(jnp.float32)
        s_loc = jnp.where(col_mask[:, None, :], s_loc, -jnp.inf).reshape(bsz, n_heads, GRID_W, n_loc)
        s_ctx = jnp.einsum("bhqd,bhkd->bhqk", q_r, k_ctx).astype(jnp.float32) * scale
        p = jax.nn.softmax(jnp.concatenate([s_loc, s_ctx], axis=-1), axis=-1).astype(v.dtype)
        return (jnp.einsum("bhqk,bhkd->bhqd", p[..., :n_loc], v_r)
                + jnp.einsum("bhqk,bhkd->bhqd", p[..., n_loc:], v_ctx))

    out = lax.map(row_block, (jnp.arange(rows), qg))
    return out.transpose(1, 0, 3, 2, 4).reshape(bsz, n_tok, n_heads * hd)


def depthwise_conv_centred(u, w, b):
    out = lax.conv_general_dilated(
        u, w[:, None, :].astype(u.dtype), window_strides=(1,),
        padding=[(D_CONV // 2, D_CONV // 2)],
        dimension_numbers=("NWC", "WIO", "NWC"), feature_group_count=u.shape[-1])
    return out + b


def ssd_chunked(xs, dt, a, b_in, c_in, init_state, with_output):
    bsz, n_tok, n_heads, hp = xs.shape
    n_groups, n_state = b_in.shape[-2:]
    hpg = n_heads // n_groups
    nc = n_tok // CHUNK
    f32 = jnp.float32
    xd = (xs.astype(f32) * dt[..., None]).reshape(bsz, nc, CHUNK, n_groups, hpg, hp)
    da = (dt * a).reshape(bsz, nc, CHUNK, n_groups, hpg)
    bc = b_in.astype(f32).reshape(bsz, nc, CHUNK, n_groups, n_state)
    cc = c_in.astype(f32).reshape(bsz, nc, CHUNK, n_groups, n_state)
    da_cs = jnp.cumsum(da, axis=2)
    da_tot = da_cs[:, :, -1]
    decay_to_end = jnp.exp(da_tot[:, :, None] - da_cs)
    chunk_states = jnp.einsum("bcqgn,bcqgh,bcqghp->bcghpn", bc, decay_to_end, xd)

    def carry_state(state, inp):
        st, tot = inp
        return state * jnp.exp(tot)[..., None, None] + st, state

    final, prev_states = lax.scan(
        carry_state, init_state.astype(f32).reshape(bsz, n_groups, hpg, hp, n_state),
        (jnp.moveaxis(chunk_states, 1, 0), jnp.moveaxis(da_tot, 1, 0)))
    final = final.reshape(bsz, n_heads, hp, n_state)
    if not with_output:
        return None, final
    prev_states = jnp.moveaxis(prev_states, 0, 1)
    tril = jnp.tril(jnp.ones((CHUNK, CHUNK), dtype=bool))
    seg = da_cs[:, :, :, None] - da_cs[:, :, None, :]
    decay = jnp.exp(jnp.where(tril[:, :, None, None], seg, -jnp.inf))
    cb = jnp.einsum("bcqgn,bckgn->bcqkg", cc, bc)
    y_diag = jnp.einsum("bcqkgh,bckghp->bcqghp", cb[..., None] * decay, xd)
    y_off = jnp.einsum("bcqgn,bcghpn,bcqgh->bcqghp", cc, prev_states, jnp.exp(da_cs))
    return (y_diag + y_off).reshape(bsz, n_tok, n_heads, hp).astype(xs.dtype), final


def ssd_bidirectional(xs, dt2, a2, b_in, c_in, state_fwd, state_bwd, with_output):
    flip = lambda u: jnp.flip(u, axis=1)
    y_f, s_f = ssd_chunked(xs, dt2[:, :, 0], a2[0], b_in, c_in, state_fwd, with_output)
    y_b, s_b = ssd_chunked(flip(xs), flip(dt2[:, :, 1]), a2[1], flip(b_in), flip(c_in),
                           state_bwd, with_output)
    y = y_f + flip(y_b) if with_output else None
    return y, s_f, s_b


def hybrid_layer(x, xc, c, c_ctx, cos, sin, ada_w, ada_b, norm_w, w_in, q_norm, k_norm, rpb,
                 conv_w, conv_b, dt_bias, a_log, d_skip, ssm_norm, w_out, update_ctx):
    bsz, n_tok, _ = x.shape
    n_ctx = xc.shape[1]
    f32 = jnp.float32
    shift, scale, gate = modulation(c, ada_w, ada_b)
    shift_c, scale_c, gate_c = modulation(c_ctx, ada_w, ada_b)
    h = rms_norm(x, norm_w) * (1 + scale[:, None]) + shift[:, None]
    hc = rms_norm(xc, norm_w) * (1 + scale_c) + shift_c

    widths = [NA_WIDTH, NA_WIDTH, NA_WIDTH, NA_WIDTH, SSM_WIDTH, CONV_CH]
    cuts = [int(s) for s in np.cumsum(widths)]
    q, k, v, g, z, xbc, dt_raw = jnp.split(h @ w_in, cuts, axis=-1)
    qc, kc, vc, gc, zc, xbcc, dt_raw_c = jnp.split(hc @ w_in, cuts, axis=-1)

    heads = lambda u: u.reshape(*u.shape[:-1], NA_HEADS, NA_HEAD_DIM)
    q = apply_rope(rms_norm(heads(q), q_norm), cos, sin)
    k = apply_rope(rms_norm(heads(k), k_norm), cos, sin)
    qc = rms_norm(heads(qc), q_norm)
    kc = rms_norm(heads(kc), k_norm)
    vc = heads(vc)
    attn = neighbourhood_attention(q, k, heads(v), kc.transpose(0, 2, 1, 3),
                                   vc.transpose(0, 2, 1, 3), rpb) * jax.nn.silu(g)

    a2 = -jnp.exp(a_log.astype(f32))
    dt2 = jax.nn.softplus(dt_raw.astype(f32).reshape(bsz, n_tok, 2, SSM_HEADS) + dt_bias.astype(f32))
    dt2c = jax.nn.softplus(dt_raw_c.astype(f32).reshape(bsz, n_ctx, 2, SSM_HEADS) + dt_bias.astype(f32))

    def ssm_inputs(u):
        u = jax.nn.silu(depthwise_conv_centred(u, conv_w, conv_b))
        xs, bs, cs = jnp.split(u, [SSM_WIDTH, SSM_WIDTH + SSM_GROUPS * SSM_STATE], axis=-1)
        lead = u.shape[:-1]
        return (xs.reshape(*lead, SSM_HEADS, SSM_HEAD_DIM),
                bs.reshape(*lead, SSM_GROUPS, SSM_STATE),
                cs.reshape(*lead, SSM_GROUPS, SSM_STATE))

    def ssm_out(y, xs, zg):
        y = y + xs * d_skip[:, None]
        return rms_norm(y.reshape(zg.shape) * jax.nn.silu(zg), ssm_norm)

    xs, bs, cs = ssm_inputs(xbc)
    xsc, bsc, csc = ssm_inputs(xbcc)
    zero = jnp.zeros((bsz, SSM_HEADS, SSM_HEAD_DIM, SSM_STATE), f32)
    yc, s_f, s_b = ssd_bidirectional(xsc, dt2c, a2, bsc, csc, zero, zero, update_ctx)
    y, _, _ = ssd_bidirectional(xs, dt2, a2, bs, cs, s_f, s_b, True)

    out = jnp.concatenate([attn, ssm_out(y, xs, z)], axis=-1) @ w_out
    x = x + gate[:, None] * out
    if update_ctx:
        attn_c = context_attention(qc, kc, vc) * jax.nn.silu(gc)
        out_c = jnp.concatenate([attn_c, ssm_out(yc, xsc, zc)], axis=-1) @ w_out
        xc = xc + gate_c * out_c
    return x, xc


def setup_inputs(seed: int = 0) -> dict:
    key = jax.random.key(seed)
    ks = jax.random.split(key, 20)
    f32 = jnp.float32
    nrm = lambda k, shape, s: jax.random.normal(k, shape, f32) * s
    dt0 = jnp.exp(jax.random.uniform(ks[13], (DEPTH, 2, SSM_HEADS), f32,
                                     math.log(1e-3), math.log(1e-1)))
    return {
        "x": nrm(ks[0], (BATCH, SEQ, D_MODEL), 1.0),
        "c": nrm(ks[1], (BATCH, D_MODEL), 1.0),
        "ctx": nrm(ks[2], (BATCH, CTX_LEN, D_MODEL), 1.0),
        "c_ctx": nrm(ks[3], (D_MODEL,), 1.0),
        "ada_w": nrm(ks[4], (DEPTH, D_MODEL, 3 * D_MODEL), 0.5 * D_MODEL ** -0.5),
        "ada_b": nrm(ks[5], (DEPTH, 3 * D_MODEL), 0.01),
        "norm_w": 1.0 + nrm(ks[6], (DEPTH, D_MODEL), 0.02),
        "w_in": nrm(ks[7], (DEPTH, D_MODEL, IN_COLS), D_MODEL ** -0.5),
        "q_norm": 1.0 + nrm(ks[8], (DEPTH, NA_HEAD_DIM), 0.02),
        "k_norm": 1.0 + nrm(ks[9], (DEPTH, NA_HEAD_DIM), 0.02),
        "rpb": nrm(ks[10], (DEPTH, NA_HEADS, 2 * WIN_H - 1, 2 * WIN_W - 1), 0.02),
        "conv_w": nrm(ks[11], (DEPTH, D_CONV, CONV_CH), D_CONV ** -0.5),
        "conv_b": nrm(ks[12], (DEPTH, CONV_CH), 0.01),
        "dt_bias": dt0 + jnp.log(-jnp.expm1(-dt0)),
        "a_log": jnp.log(jax.random.uniform(ks[14], (DEPTH, 2, SSM_HEADS), f32, 1.0, 16.0)),
        "d_skip": 1.0 + nrm(ks[15], (DEPTH, SSM_HEADS), 0.1),
        "ssm_norm": 1.0 + nrm(ks[16], (DEPTH, SSM_WIDTH), 0.02),
        "w_out": nrm(ks[17], (DEPTH, MIX_WIDTH, D_MODEL), MIX_WIDTH ** -0.5),
    }


def reference(x, c, ctx, c_ctx, ada_w, ada_b, norm_w, w_in, q_norm, k_norm, rpb, conv_w, conv_b,
              dt_bias, a_log, d_skip, ssm_norm, w_out):
    cos, sin = axial_rope_tables(x.shape[1], NA_HEAD_DIM)
    xc = ctx
    for layer in range(DEPTH):
        x, xc = hybrid_layer(
            x, xc, c, c_ctx, cos, sin, ada_w[layer], ada_b[layer], norm_w[layer], w_in[layer],
            q_norm[layer], k_norm[layer], rpb[layer], conv_w[layer], conv_b[layer],
            dt_bias[layer], a_log[layer], d_skip[layer], ssm_norm[layer], w_out[layer],
            update_ctx=layer < DEPTH - 1)
    return x
```

```python
import functools

import jax
import jax.numpy as jnp
import numpy as np
from jax import lax
from jax.experimental import pallas as pl
from jax.experimental.pallas import tpu as pltpu

F32 = jnp.float32
BF16 = jnp.bfloat16

D_MODEL = 2048
GRID_W = 64
EPS = 1e-6
HEAD_DIM = 128
NA_HEADS = D_MODEL // HEAD_DIM
NA_WIDTH = NA_HEADS * HEAD_DIM
WIN_H = 8
WIN_W = 16
ROPE_THETA = 10000.0
SSM_WIDTH = D_MODEL
SSM_HEAD_DIM = 64
SSM_HEADS = SSM_WIDTH // SSM_HEAD_DIM
SSM_STATE = 128
SSM_GROUPS = 8
HEADS_PER_GROUP = SSM_HEADS // SSM_GROUPS
D_CONV = 5
CHUNK = 128
BC_WIDTH = 2 * SSM_GROUPS * SSM_STATE
MAIN_COLS = 4 * NA_WIDTH + SSM_WIDTH + SSM_WIDTH + BC_WIDTH
DT_PAD = 128
LANES = 128
HALO = 16
Q_ROWS = 4
Q_BLK = Q_ROWS * GRID_W
K_ROWS = Q_ROWS + WIN_H
K_BLK = K_ROWS * GRID_W
NEG = -1e30
VMEM_LIMIT = 56 * 1024 * 1024

COL_Q, COL_K, COL_V, COL_G, COL_Z, COL_X, COL_BC = 0, 1, 2, 3, 4, 5, 6


def _silu(u):
    return u * (1.0 / (1.0 + jnp.exp(-u)))


def _params(*sem):
    return pltpu.CompilerParams(dimension_semantics=sem, vmem_limit_bytes=VMEM_LIMIT)


def _mod_kernel(c_ref, w_ref, b_ref, o_ref):
    s = _silu(c_ref[...])
    o_ref[0] = jnp.dot(s, w_ref[0], preferred_element_type=F32) + b_ref[0]


def _modulation(cond, ada_w, ada_b):
    depth, d, n = ada_w.shape
    rows = cond.shape[0]
    tn = 512
    return pl.pallas_call(
        _mod_kernel,
        grid=(depth, n // tn),
        in_specs=[
            pl.BlockSpec((rows, d), lambda l, j: (0, 0)),
            pl.BlockSpec((1, d, tn), lambda l, j: (l, 0, j)),
            pl.BlockSpec((1, 1, tn), lambda l, j: (l, 0, j)),
        ],
        out_specs=pl.BlockSpec((1, rows, tn), lambda l, j: (l, 0, j)),
        out_shape=jax.ShapeDtypeStruct((depth, rows, n), F32),
        compiler_params=_params("arbitrary", "arbitrary"),
        name="modulation",
    )(cond, ada_w, ada_b.reshape(depth, 1, n))


def _inproj_kernel(x_ref, sc_ref, sh_ref, nw_ref, w_ref, wdt_ref, qn_ref, kn_ref, cos_ref, sin_ref,
                   o_ref, dt_ref, h_ref, *, qk_blocks, heads_per_block):
    j = pl.program_id(1)

    @pl.when(j == 0)
    def _():
        xf = x_ref[...]
        ms = jnp.mean(xf * xf, axis=-1, keepdims=True)
        y = xf * lax.rsqrt(ms + EPS) * nw_ref[...]
        hb = (y * sc_ref[0] + sh_ref[0]).astype(BF16)
        h_ref[...] = hb
        dt_ref[...] = jnp.dot(hb, wdt_ref[...], preferred_element_type=F32)

    acc = jnp.dot(h_ref[...], w_ref[...], preferred_element_type=F32)

    @pl.when(j < 2 * qk_blocks)
    def _():
        nrm = jnp.where(j < qk_blocks, qn_ref[...], kn_ref[...])
        cos = cos_ref[...]
        sin = sin_ref[...]
        even = (lax.broadcasted_iota(jnp.int32, cos.shape, 1) % 2) == 0
        for hh in range(heads_per_block):
            u = acc[:, hh * HEAD_DIM:(hh + 1) * HEAD_DIM]
            ms = jnp.mean(u * u, axis=-1, keepdims=True)
            u = u * lax.rsqrt(ms + EPS) * nrm
            partner = jnp.where(even, pltpu.roll(u, HEAD_DIM - 1, 1), pltpu.roll(u, 1, 1))
            o_ref[:, hh * HEAD_DIM:(hh + 1) * HEAD_DIM] = (u * cos + partner * sin).astype(o_ref.dtype)

    @pl.when(j >= 2 * qk_blocks)
    def _():
        o_ref[...] = acc.astype(o_ref.dtype)


def _inproj(x2d, scale1, shift, norm_w, w_main, w_dt, q_norm, k_norm, cos_t, sin_t, *, tm, seq_len):
    m, d = x2d.shape
    n = w_main.shape[1]
    tn = 512
    blocks_per_seq = seq_len // tm
    tab_blocks = cos_t.shape[0] // tm
    per_batch = scale1.shape[0] > 1
    mod_map = (lambda i, j: (i // blocks_per_seq, 0, 0)) if per_batch else (lambda i, j: (0, 0, 0))
    kern = functools.partial(_inproj_kernel, qk_blocks=NA_WIDTH // tn, heads_per_block=tn // HEAD_DIM)
    return pl.pallas_call(
        kern,
        grid=(m // tm, n // tn),
        in_specs=[
            pl.BlockSpec((tm, d), lambda i, j: (i, 0)),
            pl.BlockSpec((1, 1, d), mod_map),
            pl.BlockSpec((1, 1, d), mod_map),
            pl.BlockSpec((1, d), lambda i, j: (0, 0)),
            pl.BlockSpec((d, tn), lambda i, j: (0, j)),
            pl.BlockSpec((d, DT_PAD), lambda i, j: (0, 0)),
            pl.BlockSpec((1, HEAD_DIM), lambda i, j: (0, 0)),
            pl.BlockSpec((1, HEAD_DIM), lambda i, j: (0, 0)),
            pl.BlockSpec((tm, HEAD_DIM), lambda i, j: (i % tab_blocks, 0)),
            pl.BlockSpec((tm, HEAD_DIM), lambda i, j: (i % tab_blocks, 0)),
        ],
        out_specs=[
            pl.BlockSpec((tm, tn), lambda i, j: (i, j)),
            pl.BlockSpec((tm, DT_PAD), lambda i, j: (i, 0)),
        ],
        out_shape=[
            jax.ShapeDtypeStruct((m, n), BF16),
            jax.ShapeDtypeStruct((m, DT_PAD), F32),
        ],
        scratch_shapes=[pltpu.VMEM((tm, d), BF16)],
        compiler_params=_params("arbitrary", "arbitrary"),
        name="inproj",
    )(x2d, scale1, shift, norm_w, w_main, w_dt, q_norm, k_norm, cos_t, sin_t)


def _nbr_attn_kernel(q_ref, k_ref, v_ref, g_ref, kc_ref, vc_ref, bias_ref, o_ref, *, n_qblk, seq_len):
    scale = HEAD_DIM ** -0.5
    kc = kc_ref[...]
    vc = vc_ref[...]
    nt = (((1,), (1,)), ((), ()))

    def body(jq, carry):
        q0 = pl.multiple_of(jq * Q_BLK, Q_BLK)
        k0 = pl.multiple_of(jnp.clip(jq * Q_BLK - (WIN_H // 2) * GRID_W, 0, seq_len - K_BLK), GRID_W)
        variant = jnp.where(jq == 0, 0, jnp.where(jq == n_qblk - 1, 2, 1))
        q = q_ref[pl.ds(q0, Q_BLK), :]
        kw = k_ref[pl.ds(k0, K_BLK), :]
        vw = v_ref[pl.ds(k0, K_BLK), :]
        s_loc = lax.dot_general(q, kw, nt, preferred_element_type=F32) * scale + bias_ref[variant, 0]
        s_ctx = lax.dot_general(q, kc, nt, preferred_element_type=F32) * scale
        mx = jnp.maximum(jnp.max(s_loc, axis=-1, keepdims=True), jnp.max(s_ctx, axis=-1, keepdims=True))
        p_loc = jnp.exp(s_loc - mx)
        p_ctx = jnp.exp(s_ctx - mx)
        den = jnp.sum(p_loc, axis=-1, keepdims=True) + jnp.sum(p_ctx, axis=-1, keepdims=True)
        o = (jnp.dot(p_loc.astype(BF16), vw, preferred_element_type=F32)
             + jnp.dot(p_ctx.astype(BF16), vc, preferred_element_type=F32))
        g = g_ref[pl.ds(q0, Q_BLK), :].astype(F32)
        o_ref[pl.ds(q0, Q_BLK), :] = (o * (1.0 / den) * _silu(g)).astype(o_ref.dtype)
        return carry

    lax.fori_loop(0, n_qblk, body, 0)


def _nbr_attention(p, pc, bias, *, bsz, seq_len, ctx_len):
    blk = NA_WIDTH // HEAD_DIM
    n_qblk = seq_len // Q_BLK
    kern = functools.partial(_nbr_attn_kernel, n_qblk=n_qblk, seq_len=seq_len)
    sec = lambda c: (lambda b, h: (b, c * blk + h))
    return pl.pallas_call(
        kern,
        grid=(bsz, NA_HEADS),
        in_specs=[
            pl.BlockSpec((seq_len, HEAD_DIM), sec(COL_Q)),
            pl.BlockSpec((seq_len, HEAD_DIM), sec(COL_K)),
            pl.BlockSpec((seq_len, HEAD_DIM), sec(COL_V)),
            pl.BlockSpec((seq_len, HEAD_DIM), sec(COL_G)),
            pl.BlockSpec((ctx_len, HEAD_DIM), sec(COL_K)),
            pl.BlockSpec((ctx_len, HEAD_DIM), sec(COL_V)),
            pl.BlockSpec((3, 1, Q_BLK, K_BLK), lambda b, h: (0, h, 0, 0)),
        ],
        out_specs=pl.BlockSpec((seq_len, HEAD_DIM), lambda b, h: (b, h)),
        out_shape=jax.ShapeDtypeStruct((bsz * seq_len, NA_WIDTH), BF16),
        compiler_params=_params("arbitrary", "arbitrary"),
        name="nbr_attention",
    )(p, p, p, p, pc, pc, bias)


def _attn_bias_tables(rpb, n_rows):
    qi = np.arange(Q_BLK)
    ki = np.arange(K_BLK)
    qr, qc = qi // GRID_W, qi % GRID_W
    kr, kcol = ki // GRID_W, ki % GRID_W
    drow = kr[None, :] - qr[:, None]
    c0 = np.clip(qc - WIN_W // 2, 0, GRID_W - WIN_W)
    col_ok = (kcol[None, :] >= c0[:, None]) & (kcol[None, :] < c0[:, None] + WIN_W)
    col_idx = np.clip(kcol[None, :] - qc[:, None] + WIN_W - 1, 0, 2 * WIN_W - 2)
    half = WIN_H // 2
    row_ok = [
        (kr[None, :] < WIN_H) & np.ones_like(drow, bool),
        (drow >= 0) & (drow < WIN_H),
        (kr[None, :] >= K_ROWS - WIN_H) & np.ones_like(drow, bool),
    ]
    row_idx = [drow + WIN_H - 1, drow - half + WIN_H - 1, drow - (K_ROWS - Q_ROWS) + WIN_H - 1]
    tabs = []
    for ok, ridx in zip(row_ok, row_idx):
        ridx = np.clip(ridx, 0, 2 * WIN_H - 2)
        vals = rpb[:, ridx, col_idx].astype(F32)
        tabs.append(jnp.where(jnp.asarray(ok & col_ok)[None], vals, NEG))
    return jnp.stack(tabs, axis=0)


def _ctx_attn_kernel(q_ref, k_ref, v_ref, g_ref, o_ref):
    scale = HEAD_DIM ** -0.5
    s = lax.dot_general(q_ref[...], k_ref[...], (((1,), (1,)), ((), ())), preferred_element_type=F32) * scale
    p = jnp.exp(s - jnp.max(s, axis=-1, keepdims=True))
    den = jnp.sum(p, axis=-1, keepdims=True)
    o = jnp.dot(p.astype(BF16), v_ref[...], preferred_element_type=F32)
    o_ref[...] = (o * (1.0 / den) * _silu(g_ref[...].astype(F32))).astype(o_ref.dtype)


def _ctx_attention(pc, *, bsz, ctx_len):
    blk = NA_WIDTH // HEAD_DIM
    sec = lambda c: (lambda b, h: (b, c * blk + h))
    return pl.pallas_call(
        _ctx_attn_kernel,
        grid=(bsz, NA_HEADS),
        in_specs=[pl.BlockSpec((ctx_len, HEAD_DIM), sec(c)) for c in (COL_Q, COL_K, COL_V, COL_G)],
        out_specs=pl.BlockSpec((ctx_len, HEAD_DIM), lambda b, h: (b, h)),
        out_shape=jax.ShapeDtypeStruct((bsz * ctx_len, NA_WIDTH), BF16),
        compiler_params=_params("arbitrary", "arbitrary"),
        name="ctx_attention",
    )(pc, pc, pc, pc)


def _split3(u):
    hi = u.astype(BF16)
    r1 = u - hi.astype(F32)
    mid = r1.astype(BF16)
    lo = (r1 - mid.astype(F32)).astype(BF16)
    return hi, mid, lo


def _ssd_kernel(*refs, n_chunks, reverse, has_init, finalize):
    it = iter(refs)
    xs_ref, bc_ref = next(it), next(it)
    xs_prev, bc_prev, xs_next, bc_next = next(it), next(it), next(it), next(it)
    dt_ref = next(it)
    cwx_ref, cwb_ref, cbx_ref, cbb_ref = next(it), next(it), next(it), next(it)
    dtb_ref, a_ref = next(it), next(it)
    init_ref = next(it) if has_init else None
    if finalize:
        z_ref, yb_ref, dskip_ref, nw_ref = next(it), next(it), next(it), next(it)
    y_ref, st_ref = next(it), next(it)
    ext_x, ext_b = next(it), next(it)

    c = pl.program_id(1)
    cc = (n_chunks - 1 - c) if reverse else c

    @pl.when(c == 0)
    def _():
        if has_init:
            st_ref[...] = init_ref[...]
        else:
            st_ref[...] = jnp.zeros(st_ref.shape, F32)

    keep_prev = (cc > 0).astype(F32)
    keep_next = (cc < n_chunks - 1).astype(F32)

    def conv(cur_ref, prev_ref, next_ref, ext_ref, w_ref, b_ref):
        ext_ref[0:HALO, :] = prev_ref[...].astype(F32) * keep_prev
        ext_ref[HALO:HALO + CHUNK, :] = cur_ref[...].astype(F32)
        ext_ref[HALO + CHUNK:2 * HALO + CHUNK, :] = next_ref[...].astype(F32) * keep_next
        acc = b_ref[...]
        for t in range(D_CONV):
            start = HALO - D_CONV // 2 + t
            acc = acc + ext_ref[start:start + CHUNK, :] * w_ref[t:t + 1, :]
        return _silu(acc)

    xs = conv(xs_ref, xs_prev, xs_next, ext_x, cwx_ref, cbx_ref)
    bcm = conv(bc_ref, bc_prev, bc_next, ext_b, cwb_ref, cbb_ref)
    xs_b = xs.astype(BF16)

    raw = dt_ref[...] + dtb_ref[...]
    dt = jnp.maximum(raw, 0.0) + jnp.log1p(jnp.exp(-jnp.abs(raw)))
    da = dt * a_ref[...]
    row = lax.broadcasted_iota(jnp.int32, (CHUNK, CHUNK), 0)
    col = lax.broadcasted_iota(jnp.int32, (CHUNK, CHUNK), 1)
    causal = (col >= row) if reverse else (col <= row)
    tri = jnp.where(causal, 1.0, 0.0).astype(BF16)
    hi, mid, lo = _split3(da)
    cs = (jnp.dot(tri, hi, preferred_element_type=F32) + jnp.dot(tri, mid, preferred_element_type=F32)
          + jnp.dot(tri, lo, preferred_element_type=F32))
    tot = cs[0:1, :] if reverse else cs[CHUNK - 1:CHUNK, :]
    ecs = jnp.exp(cs)
    wgt = jnp.exp(tot - cs) * dt
    etot = jnp.exp(tot)
    cs_t = cs.T
    dt_t = dt.T
    lane = lax.broadcasted_iota(jnp.int32, (CHUNK, LANES), 1)
    low_half = lane < SSM_HEAD_DIM
    lane_row = lax.broadcasted_iota(jnp.int32, (1, LANES), 1) < SSM_HEAD_DIM
    off = SSM_HEADS if reverse else 0

    y_parts = []
    for g in range(SSM_GROUPS):
        b_g = bcm[:, g * SSM_STATE:(g + 1) * SSM_STATE]
        c_g = bcm[:, (SSM_GROUPS + g) * SSM_STATE:(SSM_GROUPS + g + 1) * SSM_STATE].astype(BF16)
        b_gt = b_g.T.astype(BF16)
        cb = jnp.dot(c_g, b_gt, preferred_element_type=F32)
        gw = HEADS_PER_GROUP * SSM_HEAD_DIM
        st_g = st_ref[0, :, g * gw:(g + 1) * gw]
        y_off = jnp.dot(c_g, st_g.astype(BF16), preferred_element_type=F32)
        new_cols = []
        for pr in range(HEADS_PER_GROUP // 2):
            h0 = g * HEADS_PER_GROUP + 2 * pr
            lanes0 = h0 * SSM_HEAD_DIM
            m_pair = []
            for h in (h0, h0 + 1):
                hc = off + h
                seg = cs[:, hc:hc + 1] - cs_t[hc:hc + 1, :]
                decay = jnp.exp(jnp.where(causal, seg, NEG))
                m_pair.append((cb * decay * dt_t[hc:hc + 1, :]).astype(BF16))
            m_cat = jnp.concatenate(m_pair, axis=1)
            x_pair = xs_b[:, lanes0:lanes0 + LANES]
            zero = jnp.zeros_like(x_pair)
            x_diag = jnp.concatenate([jnp.where(low_half, x_pair, zero), jnp.where(low_half, zero, x_pair)], axis=0)
            y_diag = jnp.dot(m_cat, x_diag, preferred_element_type=F32)
            e_pair = jnp.where(low_half, ecs[:, off + h0:off + h0 + 1], ecs[:, off + h0 + 1:off + h0 + 2])
            w_pair = jnp.where(low_half, wgt[:, off + h0:off + h0 + 1], wgt[:, off + h0 + 1:off + h0 + 2])
            t_pair = jnp.where(lane_row, etot[:, off + h0:off + h0 + 1], etot[:, off + h0 + 1:off + h0 + 2])
            y_parts.append(y_diag + y_off[:, 2 * pr * SSM_HEAD_DIM:2 * pr * SSM_HEAD_DIM + LANES] * e_pair)
            xw = (xs[:, lanes0:lanes0 + LANES] * w_pair).astype(BF16)
            upd = jnp.dot(b_gt, xw, preferred_element_type=F32)
            new_cols.append(st_g[:, 2 * pr * SSM_HEAD_DIM:2 * pr * SSM_HEAD_DIM + LANES] * t_pair + upd)
        st_ref[0, :, g * gw:(g + 1) * gw] = jnp.concatenate(new_cols, axis=1)
    y = jnp.concatenate(y_parts, axis=1)

    if finalize:
        y = y + yb_ref[...].astype(F32) + xs * dskip_ref[...]
        y = y * _silu(z_ref[...].astype(F32))
        ms = jnp.mean(y * y, axis=-1, keepdims=True)
        y = y * lax.rsqrt(ms + EPS) * nw_ref[...]
    y_ref[...] = y.astype(y_ref.dtype)


def _ssd_pass(p, dt_raw, consts, init_state, y_bwd, *, bsz, seq_len, reverse):
    n_chunks = seq_len // CHUNK
    has_init = init_state is not None
    finalize = y_bwd is not None
    m = bsz * seq_len
    wide = NA_WIDTH
    per_chunk = CHUNK // HALO
    last_halo = m // HALO - 1

    def chunk_of(b, c):
        return b * n_chunks + ((n_chunks - 1 - c) if reverse else c)

    cur = lambda col: (lambda b, c: (chunk_of(b, c), col))
    prev = lambda col: (lambda b, c: (jnp.maximum(chunk_of(b, c) * per_chunk - 1, 0), col))
    nxt = lambda col: (lambda b, c: (jnp.minimum(chunk_of(b, c) * per_chunk + per_chunk, last_halo), col))
    const2 = lambda b, c: (0, 0)

    in_specs = [
        pl.BlockSpec((CHUNK, wide), cur(COL_X)), pl.BlockSpec((CHUNK, wide), cur(COL_BC)),
        pl.BlockSpec((HALO, wide), prev(COL_X)), pl.BlockSpec((HALO, wide), prev(COL_BC)),
        pl.BlockSpec((HALO, wide), nxt(COL_X)), pl.BlockSpec((HALO, wide), nxt(COL_BC)),
        pl.BlockSpec((CHUNK, DT_PAD), cur(0)),
        pl.BlockSpec((8, SSM_WIDTH), const2), pl.BlockSpec((8, BC_WIDTH), const2),
        pl.BlockSpec((1, SSM_WIDTH), const2), pl.BlockSpec((1, BC_WIDTH), const2),
        pl.BlockSpec((1, DT_PAD), const2), pl.BlockSpec((1, DT_PAD), const2),
    ]
    args = [p, p, p, p, p, p, dt_raw, consts["conv_w_x"], consts["conv_w_bc"], consts["conv_b_x"],
            consts["conv_b_bc"], consts["dt_bias"], consts["a"]]
    if has_init:
        in_specs.append(pl.BlockSpec((1, SSM_STATE, SSM_WIDTH), lambda b, c: (b, 0, 0)))
        args.append(init_state)
    if finalize:
        in_specs += [
            pl.BlockSpec((CHUNK, wide), cur(COL_Z)),
            pl.BlockSpec((CHUNK, SSM_WIDTH), cur(0)),
            pl.BlockSpec((1, SSM_WIDTH), const2), pl.BlockSpec((1, SSM_WIDTH), const2),
        ]
        args += [p, y_bwd, consts["d_skip"], consts["ssm_norm"]]
    kern = functools.partial(_ssd_kernel, n_chunks=n_chunks, reverse=reverse, has_init=has_init,
                             finalize=finalize)
    return pl.pallas_call(
        kern,
        grid=(bsz, n_chunks),
        in_specs=in_specs,
        out_specs=[
            pl.BlockSpec((CHUNK, SSM_WIDTH), cur(0)),
            pl.BlockSpec((1, SSM_STATE, SSM_WIDTH), lambda b, c: (b, 0, 0)),
        ],
        out_shape=[
            jax.ShapeDtypeStruct((m, SSM_WIDTH), BF16),
            jax.ShapeDtypeStruct((bsz, SSM_STATE, SSM_WIDTH), F32),
        ],
        scratch_shapes=[
            pltpu.VMEM((2 * HALO + CHUNK, SSM_WIDTH), F32),
            pltpu.VMEM((2 * HALO + CHUNK, BC_WIDTH), F32),
        ],
        compiler_params=_params("arbitrary", "arbitrary"),
        name="ssd_bwd" if reverse else "ssd_fwd",
    )(*args)


def _outproj_kernel(a_ref, s_ref, wa_ref, ws_ref, x_ref, g_ref, o_ref):
    acc = (jnp.dot(a_ref[...], wa_ref[...], preferred_element_type=F32)
           + jnp.dot(s_ref[...], ws_ref[...], preferred_element_type=F32))
    o_ref[...] = x_ref[...] + g_ref[0] * acc


def _outproj(attn, ssm, w_out, x2d, gate, *, tm, seq_len):
    m, d = x2d.shape
    tn = 1024
    blocks_per_seq = seq_len // tm
    per_batch = gate.shape[0] > 1
    mod_map = (lambda i, j: (i // blocks_per_seq, 0, j)) if per_batch else (lambda i, j: (0, 0, j))
    return pl.pallas_call(
        _outproj_kernel,
        grid=(m // tm, d // tn),
        in_specs=[
            pl.BlockSpec((tm, NA_WIDTH), lambda i, j: (i, 0)),
            pl.BlockSpec((tm, SSM_WIDTH), lambda i, j: (i, 0)),
            pl.BlockSpec((NA_WIDTH, tn), lambda i, j: (0, j)),
            pl.BlockSpec((SSM_WIDTH, tn), lambda i, j: (1, j)),
            pl.BlockSpec((tm, tn), lambda i, j: (i, j)),
            pl.BlockSpec((1, 1, tn), mod_map),
        ],
        out_specs=pl.BlockSpec((tm, tn), lambda i, j: (i, j)),
        out_shape=jax.ShapeDtypeStruct((m, d), F32),
        compiler_params=_params("arbitrary", "arbitrary"),
        name="outproj",
    )(attn, ssm, w_out, w_out, x2d, gate)


def _rope_tables(n_tokens):
    t = jnp.arange(n_tokens)
    rows = (t // GRID_W).astype(F32)
    cols = (t % GRID_W).astype(F32)
    n_pairs = HEAD_DIM // 4
    freqs = ROPE_THETA ** (-jnp.arange(n_pairs, dtype=F32) / n_pairs)
    ang = jnp.concatenate([rows[:, None] * freqs, cols[:, None] * freqs], axis=-1)
    cos, sin = jnp.cos(ang), jnp.sin(ang)
    cos_rep = jnp.repeat(cos, 2, axis=-1)
    sin_signed = jnp.stack([-sin, sin], axis=-1).reshape(n_tokens, HEAD_DIM)
    return cos_rep, sin_signed


def kernel(x, c, ctx, c_ctx, ada_w, ada_b, norm_w, w_in, q_norm, k_norm, rpb, conv_w, conv_b, dt_bias,
           a_log, d_skip, ssm_norm, w_out):
    bsz, seq_len, d = x.shape
    ctx_len = ctx.shape[1]
    depth = ada_w.shape[0]
    assert d == D_MODEL and seq_len % 1024 == 0 and seq_len >= K_BLK + Q_BLK and ctx_len % CHUNK == 0

    cond = jnp.zeros((8, d), F32).at[:bsz].set(c).at[bsz].set(c_ctx)
    mod = _modulation(cond, ada_w, ada_b)

    cos_x, sin_x = _rope_tables(seq_len)
    cos_c = jnp.ones((ctx_len, HEAD_DIM), F32)
    sin_c = jnp.zeros((ctx_len, HEAD_DIM), F32)

    w_main = w_in[:, :, :MAIN_COLS].astype(BF16)
    w_dt = jnp.pad(w_in[:, :, MAIN_COLS:], ((0, 0), (0, 0), (0, DT_PAD - 2 * SSM_HEADS))).astype(BF16)
    w_out_b = w_out.astype(BF16)
    pad_dt = lambda u, v: jnp.pad(u.reshape(1, 2 * SSM_HEADS), ((0, 0), (0, DT_PAD - 2 * SSM_HEADS)),
                                  constant_values=v)

    xs2 = x.reshape(bsz * seq_len, d)
    xc2 = ctx.reshape(bsz * ctx_len, d)
    tm_x = 1024
    tm_c = ctx_len

    for l in range(depth):
        update_ctx = l < depth - 1
        shift, scale, gate = mod[l, :, :d], mod[l, :, d:2 * d], mod[l, :, 2 * d:]
        sc_x, sh_x, gt_x = (1.0 + scale[:bsz])[:, None], shift[:bsz][:, None], gate[:bsz][:, None]
        sc_c, sh_c, gt_c = ((1.0 + scale[bsz:bsz + 1])[:, None], shift[bsz:bsz + 1][:, None],
                            gate[bsz:bsz + 1][:, None])
        nw = norm_w[l].reshape(1, d)
        qn, kn = q_norm[l].reshape(1, HEAD_DIM), k_norm[l].reshape(1, HEAD_DIM)
        cw = jnp.pad(conv_w[l], ((0, 8 - D_CONV), (0, 0)))
        consts = {
            "conv_w_x": cw[:, :SSM_WIDTH], "conv_w_bc": cw[:, SSM_WIDTH:],
            "conv_b_x": conv_b[l][None, :SSM_WIDTH], "conv_b_bc": conv_b[l][None, SSM_WIDTH:],
            "dt_bias": pad_dt(dt_bias[l], 0.0), "a": pad_dt(-jnp.exp(a_log[l].astype(F32)), -1.0),
            "d_skip": jnp.repeat(d_skip[l], SSM_HEAD_DIM)[None, :], "ssm_norm": ssm_norm[l][None, :],
        }

        pc, dtc = _inproj(xc2, sc_c, sh_c, nw, w_main[l], w_dt[l], qn, kn, cos_c, sin_c,
                          tm=tm_c, seq_len=ctx_len)
        p, dtx = _inproj(xs2, sc_x, sh_x, nw, w_main[l], w_dt[l], qn, kn, cos_x, sin_x,
                         tm=tm_x, seq_len=seq_len)

        ycb, s_b = _ssd_pass(pc, dtc, consts, None, None, bsz=bsz, seq_len=ctx_len, reverse=True)
        ssm_c, s_f = _ssd_pass(pc, dtc, consts, None, ycb, bsz=bsz, seq_len=ctx_len, reverse=False)
        yb, _ = _ssd_pass(p, dtx, consts, s_b, None, bsz=bsz, seq_len=seq_len, reverse=True)
        ssm_x, _ = _ssd_pass(p, dtx, consts, s_f, yb, bsz=bsz, seq_len=seq_len, reverse=False)

        bias = _attn_bias_tables(rpb[l], seq_len // GRID_W)
        attn_x = _nbr_attention(p, pc, bias, bsz=bsz, seq_len=seq_len, ctx_len=ctx_len)
        xs2 = _outproj(attn_x, ssm_x, w_out_b[l], xs2, gt_x, tm=512, seq_len=seq_len)
        if update_ctx:
            attn_c = _ctx_attention(pc, bsz=bsz, ctx_len=ctx_len)
            xc2 = _outproj(attn_c, ssm_c, w_out_b[l], xc2, gt_c, tm=ctx_len, seq_len=ctx_len)

    return xs2.reshape(bsz, seq_len, d)
```

```python
import functools

import jax
import jax.numpy as jnp
import numpy as np
from jax import lax
from jax.experimental import pallas as pl
from jax.experimental.pallas import tpu as pltpu

F32 = jnp.float32
BF16 = jnp.bfloat16

D_MODEL = 2048
GRID_W = 64
EPS = 1e-6
HEAD_DIM = 128
NA_HEADS = D_MODEL // HEAD_DIM
NA_WIDTH = NA_HEADS * HEAD_DIM
WIN_H = 8
WIN_W = 16
ROPE_THETA = 10000.0
SSM_WIDTH = D_MODEL
SSM_HEAD_DIM = 64
SSM_HEADS = SSM_WIDTH // SSM_HEAD_DIM
SSM_STATE = 128
SSM_GROUPS = 8
HEADS_PER_GROUP = SSM_HEADS // SSM_GROUPS
D_CONV = 5
CHUNK = 128
BC_WIDTH = 2 * SSM_GROUPS * SSM_STATE
MAIN_COLS = 4 * NA_WIDTH + SSM_WIDTH + SSM_WIDTH + BC_WIDTH
DT_PAD = 128
LANES = 128
HALO = 16
SSD_SEQS_PER_STEP = 2
N_SSD_FIXED_INPUTS = 7
EXT_ROWS = 256
LOG2E = 1.4426950408889634
Q_ROWS = 4
Q_BLK = Q_ROWS * GRID_W
K_ROWS = Q_ROWS + WIN_H
K_BLK = K_ROWS * GRID_W
NEG = -1e30
VMEM_LIMIT = 56 * 1024 * 1024

COL_Q, COL_K, COL_V, COL_G, COL_Z, COL_X, COL_BC = 0, 1, 2, 3, 4, 5, 6


def _silu(u):
    h = 0.5 * u
    return h + h * jnp.tanh(h)


def _params(*sem):
    return pltpu.CompilerParams(dimension_semantics=sem, vmem_limit_bytes=VMEM_LIMIT)


def _mod_kernel(c_ref, w_ref, b_ref, o_ref):
    s = _silu(c_ref[...])
    o_ref[0] = jnp.dot(s, w_ref[0], preferred_element_type=F32) + b_ref[0]


def _modulation(cond, ada_w, ada_b):
    depth, d, n = ada_w.shape
    rows = cond.shape[0]
    tn = 512
    return pl.pallas_call(
        _mod_kernel,
        grid=(depth, n // tn),
        in_specs=[
            pl.BlockSpec((rows, d), lambda l, j: (0, 0)),
            pl.BlockSpec((1, d, tn), lambda l, j: (l, 0, j)),
            pl.BlockSpec((1, 1, tn), lambda l, j: (l, 0, j)),
        ],
        out_specs=pl.BlockSpec((1, rows, tn), lambda l, j: (l, 0, j)),
        out_shape=jax.ShapeDtypeStruct((depth, rows, n), F32),
        compiler_params=_params("arbitrary", "arbitrary"),
        name="modulation",
    )(cond, ada_w, ada_b.reshape(depth, 1, n))


def _inproj_kernel(x_ref, sc_ref, sh_ref, nw_ref, w_ref, wdt_ref, qn_ref, kn_ref, cos_ref, sin_ref,
                   o_ref, dt_ref, h_ref, *, qk_blocks, heads_per_block):
    j = pl.program_id(1)

    @pl.when(j == 0)
    def _():
        xf = x_ref[...]
        ms = jnp.mean(xf * xf, axis=-1, keepdims=True)
        y = xf * lax.rsqrt(ms + EPS) * nw_ref[...]
        hb = (y * sc_ref[0] + sh_ref[0]).astype(BF16)
        h_ref[...] = hb
        dt_ref[...] = jnp.dot(hb, wdt_ref[...], preferred_element_type=F32)

    acc = jnp.dot(h_ref[...], w_ref[...], preferred_element_type=F32)

    @pl.when(j < 2 * qk_blocks)
    def _():
        nrm = jnp.where(j < qk_blocks, qn_ref[...], kn_ref[...])
        cos = cos_ref[...]
        sin = sin_ref[...]
        even = (lax.broadcasted_iota(jnp.int32, cos.shape, 1) % 2) == 0
        for hh in range(heads_per_block):
            u = acc[:, hh * HEAD_DIM:(hh + 1) * HEAD_DIM]
            ms = jnp.mean(u * u, axis=-1, keepdims=True)
            u = u * lax.rsqrt(ms + EPS) * nrm
            partner = jnp.where(even, pltpu.roll(u, HEAD_DIM - 1, 1), pltpu.roll(u, 1, 1))
            o_ref[:, hh * HEAD_DIM:(hh + 1) * HEAD_DIM] = (u * cos + partner * sin).astype(o_ref.dtype)

    @pl.when(j >= 2 * qk_blocks)
    def _():
        o_ref[...] = acc.astype(o_ref.dtype)


def _inproj(x2d, scale1, shift, norm_w, w_main, w_dt, q_norm, k_norm, cos_t, sin_t, *, tm, seq_len):
    m, d = x2d.shape
    n = w_main.shape[1]
    tn = 1024
    blocks_per_seq = seq_len // tm
    tab_blocks = cos_t.shape[0] // tm
    per_batch = scale1.shape[0] > 1
    mod_map = (lambda i, j: (i // blocks_per_seq, 0, 0)) if per_batch else (lambda i, j: (0, 0, 0))
    kern = functools.partial(_inproj_kernel, qk_blocks=NA_WIDTH // tn, heads_per_block=tn // HEAD_DIM)
    return pl.pallas_call(
        kern,
        grid=(m // tm, n // tn),
        in_specs=[
            pl.BlockSpec((tm, d), lambda i, j: (i, 0)),
            pl.BlockSpec((1, 1, d), mod_map),
            pl.BlockSpec((1, 1, d), mod_map),
            pl.BlockSpec((1, d), lambda i, j: (0, 0)),
            pl.BlockSpec((d, tn), lambda i, j: (0, j)),
            pl.BlockSpec((d, DT_PAD), lambda i, j: (0, 0)),
            pl.BlockSpec((1, HEAD_DIM), lambda i, j: (0, 0)),
            pl.BlockSpec((1, HEAD_DIM), lambda i, j: (0, 0)),
            pl.BlockSpec((tm, HEAD_DIM), lambda i, j: (i % tab_blocks, 0)),
            pl.BlockSpec((tm, HEAD_DIM), lambda i, j: (i % tab_blocks, 0)),
        ],
        out_specs=[
            pl.BlockSpec((tm, tn), lambda i, j: (i, j)),
            pl.BlockSpec((tm, DT_PAD), lambda i, j: (i, 0)),
        ],
        out_shape=[
            jax.ShapeDtypeStruct((m, n), BF16),
            jax.ShapeDtypeStruct((m, DT_PAD), F32),
        ],
        scratch_shapes=[pltpu.VMEM((tm, d), BF16)],
        compiler_params=_params("arbitrary", "arbitrary"),
        name="inproj",
    )(x2d, scale1, shift, norm_w, w_main, w_dt, q_norm, k_norm, cos_t, sin_t)


def _nbr_attn_kernel(q_ref, k_ref, v_ref, g_ref, kc_ref, vc_ref, bias_ref, o_ref, *, n_qblk, seq_len):
    kc = kc_ref[...]
    vc = vc_ref[...]
    nt = (((1,), (1,)), ((), ()))

    def body(jq, carry):
        q0 = pl.multiple_of(jq * Q_BLK, Q_BLK)
        k0 = pl.multiple_of(jnp.clip(jq * Q_BLK - (WIN_H // 2) * GRID_W, 0, seq_len - K_BLK), GRID_W)
        variant = jnp.where(jq == 0, 0, jnp.where(jq == n_qblk - 1, 2, 1))
        q = q_ref[pl.ds(q0, Q_BLK), :]
        kw = k_ref[pl.ds(k0, K_BLK), :]
        vw = v_ref[pl.ds(k0, K_BLK), :]
        s_loc = lax.dot_general(q, kw, nt, preferred_element_type=F32) + bias_ref[variant, 0]
        s_ctx = lax.dot_general(q, kc, nt, preferred_element_type=F32)
        mx = jnp.maximum(jnp.max(s_loc, axis=-1, keepdims=True), jnp.max(s_ctx, axis=-1, keepdims=True))
        p_loc = jnp.exp2(s_loc - mx)
        p_ctx = jnp.exp2(s_ctx - mx)
        den = jnp.sum(p_loc, axis=-1, keepdims=True) + jnp.sum(p_ctx, axis=-1, keepdims=True)
        o = (jnp.dot(p_loc.astype(BF16), vw, preferred_element_type=F32)
             + jnp.dot(p_ctx.astype(BF16), vc, preferred_element_type=F32))
        g = g_ref[pl.ds(q0, Q_BLK), :].astype(F32)
        o_ref[pl.ds(q0, Q_BLK), :] = (o * (1.0 / den) * _silu(g)).astype(o_ref.dtype)
        return carry

    lax.fori_loop(0, n_qblk, body, 0, unroll=2)


def _nbr_attention(p, pc, bias, *, bsz, seq_len, ctx_len):
    blk = NA_WIDTH // HEAD_DIM
    n_qblk = seq_len // Q_BLK
    kern = functools.partial(_nbr_attn_kernel, n_qblk=n_qblk, seq_len=seq_len)
    sec = lambda c: (lambda b, h: (b, c * blk + h))
    return pl.pallas_call(
        kern,
        grid=(bsz, NA_HEADS),
        in_specs=[
            pl.BlockSpec((seq_len, HEAD_DIM), sec(COL_Q)),
            pl.BlockSpec((seq_len, HEAD_DIM), sec(COL_K)),
            pl.BlockSpec((seq_len, HEAD_DIM), sec(COL_V)),
            pl.BlockSpec((seq_len, HEAD_DIM), sec(COL_G)),
            pl.BlockSpec((ctx_len, HEAD_DIM), sec(COL_K)),
            pl.BlockSpec((ctx_len, HEAD_DIM), sec(COL_V)),
            pl.BlockSpec((3, 1, Q_BLK, K_BLK), lambda b, h: (0, h, 0, 0)),
        ],
        out_specs=pl.BlockSpec((seq_len, HEAD_DIM), lambda b, h: (b, h)),
        out_shape=jax.ShapeDtypeStruct((bsz * seq_len, NA_WIDTH), BF16),
        compiler_params=_params("arbitrary", "arbitrary"),
        name="nbr_attention",
    )(p, p, p, p, pc, pc, bias)


def _attn_bias_tables(rpb, n_rows):
    qi = np.arange(Q_BLK)
    ki = np.arange(K_BLK)
    qr, qc = qi // GRID_W, qi % GRID_W
    kr, kcol = ki // GRID_W, ki % GRID_W
    drow = kr[None, :] - qr[:, None]
    c0 = np.clip(qc - WIN_W // 2, 0, GRID_W - WIN_W)
    col_ok = (kcol[None, :] >= c0[:, None]) & (kcol[None, :] < c0[:, None] + WIN_W)
    col_idx = np.clip(kcol[None, :] - qc[:, None] + WIN_W - 1, 0, 2 * WIN_W - 2)
    half = WIN_H // 2
    row_ok = [
        (kr[None, :] < WIN_H) & np.ones_like(drow, bool),
        (drow >= 0) & (drow < WIN_H),
        (kr[None, :] >= K_ROWS - WIN_H) & np.ones_like(drow, bool),
    ]
    row_idx = [drow + WIN_H - 1, drow - half + WIN_H - 1, drow - (K_ROWS - Q_ROWS) + WIN_H - 1]
    n_r, n_c = 2 * WIN_H - 1, 2 * WIN_W - 1
    col_oh = np.zeros((GRID_W, GRID_W, n_c), np.float32)
    col_oh[np.arange(GRID_W)[:, None], np.arange(GRID_W)[None, :], col_idx[:GRID_W, :GRID_W]] = 1.0
    row_oh = np.zeros((3, Q_ROWS, K_ROWS, n_r), np.float32)
    for v, ridx in enumerate(row_idx):
        r = np.clip(ridx[::GRID_W, ::GRID_W], 0, n_r - 1)
        row_oh[v, np.arange(Q_ROWS)[:, None], np.arange(K_ROWS)[None, :], r] = 1.0
    vals = jnp.einsum("vqki,hij,cdj->vhqckd", jnp.asarray(row_oh), rpb.astype(F32) * LOG2E,
                      jnp.asarray(col_oh), precision=lax.Precision.HIGHEST)
    vals = vals.reshape(3, NA_HEADS, Q_BLK, K_BLK)
    ok = np.stack([r & col_ok for r in row_ok], axis=0)[:, None]
    return jnp.where(jnp.asarray(ok), vals, NEG)


def _ctx_attn_kernel(q_ref, k_ref, v_ref, g_ref, o_ref):
    s = lax.dot_general(q_ref[...], k_ref[...], (((1,), (1,)), ((), ())), preferred_element_type=F32)
    p = jnp.exp2(s - jnp.max(s, axis=-1, keepdims=True))
    den = jnp.sum(p, axis=-1, keepdims=True)
    o = jnp.dot(p.astype(BF16), v_ref[...], preferred_element_type=F32)
    o_ref[...] = (o * (1.0 / den) * _silu(g_ref[...].astype(F32))).astype(o_ref.dtype)


def _ctx_attention(pc, *, bsz, ctx_len):
    blk = NA_WIDTH // HEAD_DIM
    sec = lambda c: (lambda b, h: (b, c * blk + h))
    return pl.pallas_call(
        _ctx_attn_kernel,
        grid=(bsz, NA_HEADS),
        in_specs=[pl.BlockSpec((ctx_len, HEAD_DIM), sec(c)) for c in (COL_Q, COL_K, COL_V, COL_G)],
        out_specs=pl.BlockSpec((ctx_len, HEAD_DIM), lambda b, h: (b, h)),
        out_shape=jax.ShapeDtypeStruct((bsz * ctx_len, NA_WIDTH), BF16),
        compiler_params=_params("arbitrary", "arbitrary"),
        name="ctx_attention",
    )(pc, pc, pc, pc)


def _split3(u):
    hi = u.astype(BF16)
    r1 = u - hi.astype(F32)
    mid = r1.astype(BF16)
    lo = (r1 - mid.astype(F32)).astype(BF16)
    return hi, mid, lo


def _ssd_prep_kernel(xs_ref, bc_ref, xs_prev, bc_prev, xs_next, bc_next, dt_ref, cwx_ref, cwb_ref, cbx_ref,
                     cbb_ref, dtb_ref, a_ref, shift_ref,
                     ax_ref, ac_ref, abt_ref, cs_ref, ecs_ref, wgt_ref, cslt_ref, *, n_chunks):
    cc = pl.program_id(1)

    raw = dt_ref[0] + dtb_ref[...]
    dt = jnp.maximum(raw, 0.0) + jnp.log1p(jnp.exp(-jnp.abs(raw)))
    da = dt * a_ref[...]
    row = lax.broadcasted_iota(jnp.int32, (CHUNK, CHUNK), 0)
    col = lax.broadcasted_iota(jnp.int32, (CHUNK, CHUNK), 1)
    tri_f = jnp.where(col <= row, 1.0, 0.0).astype(BF16)
    tri_b = jnp.where(col >= row, 1.0, 0.0).astype(BF16)
    parts = _split3(da)
    cs_f = sum(jnp.dot(tri_f, u, preferred_element_type=F32) for u in parts)
    cs_b = sum(jnp.dot(tri_b, u, preferred_element_type=F32) for u in parts)
    fwd_lane = lax.broadcasted_iota(jnp.int32, (CHUNK, DT_PAD), 1) < SSM_HEADS
    cs = jnp.where(fwd_lane, cs_f, cs_b)
    tot = jnp.where(fwd_lane[0:1], cs[CHUNK - 1:CHUNK, :], cs[0:1, :])
    cs_ref[0] = cs
    ecs_ref[0] = jnp.exp(cs)
    wgt_ref[0] = jnp.exp(tot - cs) * dt
    cslt_ref[0] = (cs - jnp.log(dt)).T

    def conv(cur_ref, prev_ref, next_ref, w_ref, b_ref):
        width = cur_ref.shape[-1]
        zero_halo = jnp.zeros((HALO, width), BF16)
        ext = jnp.concatenate([
            jnp.where(cc > 0, prev_ref[0], zero_halo), cur_ref[0],
            jnp.where(cc < n_chunks - 1, next_ref[0], zero_halo),
            jnp.zeros((EXT_ROWS - 2 * HALO - CHUNK, width), BF16)], axis=0)
        taps = jnp.dot(shift_ref[...], ext, preferred_element_type=F32)
        acc = b_ref[...]
        for t in range(D_CONV):
            acc = acc + taps[t * CHUNK:(t + 1) * CHUNK] * w_ref[t:t + 1, :]
        return _silu(acc)

    ax_ref[0] = conv(xs_ref, xs_prev, xs_next, cwx_ref, cbx_ref).astype(BF16)
    bcm = conv(bc_ref, bc_prev, bc_next, cwb_ref, cbb_ref)
    half = SSM_GROUPS * SSM_STATE
    ac_ref[0] = bcm[:, half:].astype(BF16)
    abt_ref[0] = bcm[:, :half].T.astype(BF16)


def _ssd_prep(p, dt_raw, consts, *, bsz, seq_len):
    n_chunks = seq_len // CHUNK
    wide = NA_WIDTH
    per_chunk = CHUNK // HALO
    last_halo = seq_len // HALO - 1
    half = SSM_GROUPS * SSM_STATE
    cur = lambda col: (lambda b, c: (b, c, col))
    prev = lambda col: (lambda b, c: (b, jnp.maximum(c * per_chunk - 1, 0), col))
    nxt = lambda col: (lambda b, c: (b, jnp.minimum(c * per_chunk + per_chunk, last_halo), col))
    const2 = lambda b, c: (0, 0)
    tab_spec = pl.BlockSpec((1, CHUNK, DT_PAD), cur(0))
    tab_shape = jax.ShapeDtypeStruct((bsz, seq_len, DT_PAD), F32)
    return pl.pallas_call(
        functools.partial(_ssd_prep_kernel, n_chunks=n_chunks),
        grid=(bsz, n_chunks),
        in_specs=[
            pl.BlockSpec((1, CHUNK, wide), cur(COL_X)), pl.BlockSpec((1, CHUNK, wide), cur(COL_BC)),
            pl.BlockSpec((1, HALO, wide), prev(COL_X)), pl.BlockSpec((1, HALO, wide), prev(COL_BC)),
            pl.BlockSpec((1, HALO, wide), nxt(COL_X)), pl.BlockSpec((1, HALO, wide), nxt(COL_BC)),
            pl.BlockSpec((1, CHUNK, DT_PAD), cur(0)),
            pl.BlockSpec((8, SSM_WIDTH), const2), pl.BlockSpec((8, BC_WIDTH), const2),
            pl.BlockSpec((1, SSM_WIDTH), const2), pl.BlockSpec((1, BC_WIDTH), const2),
            pl.BlockSpec((1, DT_PAD), const2), pl.BlockSpec((1, DT_PAD), const2),
            pl.BlockSpec((D_CONV * CHUNK, EXT_ROWS), const2),
        ],
        out_specs=[
            pl.BlockSpec((1, CHUNK, SSM_WIDTH), cur(0)),
            pl.BlockSpec((1, CHUNK, half), cur(0)),
            pl.BlockSpec((1, half, CHUNK), cur(0)),
            tab_spec, tab_spec, tab_spec, tab_spec,
        ],
        out_shape=[
            jax.ShapeDtypeStruct((bsz, seq_len, SSM_WIDTH), BF16),
            jax.ShapeDtypeStruct((bsz, seq_len, half), BF16),
            jax.ShapeDtypeStruct((bsz, n_chunks * half, CHUNK), BF16),
            tab_shape, tab_shape, tab_shape, tab_shape,
        ],
        compiler_params=_params("arbitrary", "arbitrary"),
        name="ssd_prep",
    )(p, p, p, p, p, p, dt_raw, consts["conv_w_x"], consts["conv_w_bc"], consts["conv_b_x"],
      consts["conv_b_bc"], consts["dt_bias"], consts["a"], consts["shift"])


def _ssd_kernel(*refs, n_seq, reverse, has_init, finalize):
    st_ref = refs[-1]
    init_ref = refs[N_SSD_FIXED_INPUTS] if has_init else None

    @pl.when(pl.program_id(1) == 0)
    def _():
        if has_init:
            st_ref[...] = init_ref[...]
        else:
            st_ref[...] = jnp.zeros(st_ref.shape, F32)

    for bi in range(n_seq):
        _ssd_chunk(bi, *refs, reverse=reverse, has_init=has_init, finalize=finalize)


def _ssd_chunk(bi, *refs, reverse, has_init, finalize):
    it = iter(refs)
    ax_ref, ac_ref, abt_ref, cs_ref, ecs_ref, wgt_ref, cslt_ref = (next(it) for _ in range(N_SSD_FIXED_INPUTS))
    init_ref = next(it) if has_init else None
    if finalize:
        z_ref, yb_ref, dskip_ref, nw_ref = next(it), next(it), next(it), next(it)
    y_ref, st_ref = next(it), next(it)

    xs_b = ax_ref[bi]
    xs = xs_b.astype(F32)
    cs = cs_ref[bi]
    ecs = ecs_ref[bi]
    wgt = wgt_ref[bi]
    csl_t = cslt_ref[bi]
    etot = ecs[0:1, :] if reverse else ecs[CHUNK - 1:CHUNK, :]
    row = lax.broadcasted_iota(jnp.int32, (CHUNK, CHUNK), 0)
    col = lax.broadcasted_iota(jnp.int32, (CHUNK, CHUNK), 1)
    causal = (col >= row) if reverse else (col <= row)
    lane = lax.broadcasted_iota(jnp.int32, (CHUNK, LANES), 1)
    low_half = lane < SSM_HEAD_DIM
    lane_row = lax.broadcasted_iota(jnp.int32, (1, LANES), 1) < SSM_HEAD_DIM
    off = SSM_HEADS if reverse else 0

    y_parts = []
    for g in range(SSM_GROUPS):
        c_g = ac_ref[bi, :, g * SSM_STATE:(g + 1) * SSM_STATE]
        b_gt = abt_ref[bi, g * SSM_STATE:(g + 1) * SSM_STATE, :]
        cb = jnp.dot(c_g, b_gt, preferred_element_type=F32)
        gw = HEADS_PER_GROUP * SSM_HEAD_DIM
        st_g = st_ref[bi, :, g * gw:(g + 1) * gw]
        y_off = jnp.dot(c_g, st_g.astype(BF16), preferred_element_type=F32)
        new_cols = []
        for pr in range(HEADS_PER_GROUP // 2):
            h0 = g * HEADS_PER_GROUP + 2 * pr
            lanes0 = h0 * SSM_HEAD_DIM
            m_pair = []
            for h in (h0, h0 + 1):
                hc = off + h
                seg = cs[:, hc:hc + 1] - csl_t[hc:hc + 1, :]
                decay = jnp.exp(jnp.where(causal, seg, NEG))
                m_pair.append((cb * decay).astype(BF16))
            m_cat = jnp.concatenate(m_pair, axis=1)
            x_pair = xs_b[:, lanes0:lanes0 + LANES]
            zero = jnp.zeros_like(x_pair)
            x_diag = jnp.concatenate([jnp.where(low_half, x_pair, zero), jnp.where(low_half, zero, x_pair)], axis=0)
            y_diag = jnp.dot(m_cat, x_diag, preferred_element_type=F32)
            e_pair = jnp.where(low_half, ecs[:, off + h0:off + h0 + 1], ecs[:, off + h0 + 1:off + h0 + 2])
            w_pair = jnp.where(low_half, wgt[:, off + h0:off + h0 + 1], wgt[:, off + h0 + 1:off + h0 + 2])
            t_pair = jnp.where(lane_row, etot[:, off + h0:off + h0 + 1], etot[:, off + h0 + 1:off + h0 + 2])
            y_parts.append(y_diag + y_off[:, 2 * pr * SSM_HEAD_DIM:2 * pr * SSM_HEAD_DIM + LANES] * e_pair)
            xw = (xs[:, lanes0:lanes0 + LANES] * w_pair).astype(BF16)
            upd = jnp.dot(b_gt, xw, preferred_element_type=F32)
            new_cols.append(st_g[:, 2 * pr * SSM_HEAD_DIM:2 * pr * SSM_HEAD_DIM + LANES] * t_pair + upd)
        st_ref[bi, :, g * gw:(g + 1) * gw] = jnp.concatenate(new_cols, axis=1)
    y = jnp.concatenate(y_parts, axis=1)

    if finalize:
        y = y + yb_ref[bi].astype(F32) + xs * dskip_ref[...]
        y = y * _silu(z_ref[bi].astype(F32))
        ms = jnp.mean(y * y, axis=-1, keepdims=True)
        y = y * lax.rsqrt(ms + EPS) * nw_ref[...]
    y_ref[bi] = y.astype(y_ref.dtype)


def _ssd_pass(p, prep, consts, init_state, y_bwd, *, bsz, seq_len, reverse):
    n_chunks = seq_len // CHUNK
    has_init = init_state is not None
    finalize = y_bwd is not None
    n_seq = SSD_SEQS_PER_STEP
    assert bsz % n_seq == 0
    wide = NA_WIDTH
    half = SSM_GROUPS * SSM_STATE

    def chunk_of(c):
        return (n_chunks - 1 - c) if reverse else c

    cur = lambda col: (lambda b, c: (b, chunk_of(c), col))
    const2 = lambda b, c: (0, 0)
    state_spec = pl.BlockSpec((n_seq, SSM_STATE, SSM_WIDTH), lambda b, c: (b, 0, 0))
    tab_spec = pl.BlockSpec((n_seq, CHUNK, DT_PAD), cur(0))

    in_specs = [
        pl.BlockSpec((n_seq, CHUNK, SSM_WIDTH), cur(0)),
        pl.BlockSpec((n_seq, CHUNK, half), cur(0)),
        pl.BlockSpec((n_seq, half, CHUNK), cur(0)),
        tab_spec, tab_spec, tab_spec, tab_spec,
    ]
    args = list(prep)
    assert len(args) == N_SSD_FIXED_INPUTS
    if has_init:
        in_specs.append(state_spec)
        args.append(init_state)
    if finalize:
        in_specs += [
            pl.BlockSpec((n_seq, CHUNK, wide), cur(COL_Z)),
            pl.BlockSpec((n_seq, CHUNK, SSM_WIDTH), cur(0)),
            pl.BlockSpec((1, SSM_WIDTH), const2), pl.BlockSpec((1, SSM_WIDTH), const2),
        ]
        args += [p, y_bwd, consts["d_skip"], consts["ssm_norm"]]
    kern = functools.partial(_ssd_kernel, n_seq=n_seq, reverse=reverse, has_init=has_init, finalize=finalize)
    return pl.pallas_call(
        kern,
        grid=(bsz // n_seq, n_chunks),
        in_specs=in_specs,
        out_specs=[pl.BlockSpec((n_seq, CHUNK, SSM_WIDTH), cur(0)), state_spec],
        out_shape=[
            jax.ShapeDtypeStruct((bsz, seq_len, SSM_WIDTH), BF16),
            jax.ShapeDtypeStruct((bsz, SSM_STATE, SSM_WIDTH), F32),
        ],
        compiler_params=_params("arbitrary", "arbitrary"),
        name="ssd_bwd" if reverse else "ssd_fwd",
    )(*args)


def _outproj_kernel(a_ref, s_ref, wa_ref, ws_ref, x_ref, g_ref, o_ref):
    acc = (jnp.dot(a_ref[...], wa_ref[...], preferred_element_type=F32)
           + jnp.dot(s_ref[...], ws_ref[...], preferred_element_type=F32))
    o_ref[...] = x_ref[...] + g_ref[0] * acc


def _outproj(attn, ssm, w_out, x2d, gate, *, tm, seq_len):
    m, d = x2d.shape
    blocks_per_seq = seq_len // tm
    per_batch = gate.shape[0] > 1
    mod_map = (lambda i: (i // blocks_per_seq, 0, 0)) if per_batch else (lambda i: (0, 0, 0))
    resident = pl.Buffered(1)
    return pl.pallas_call(
        _outproj_kernel,
        grid=(m // tm,),
        in_specs=[
            pl.BlockSpec((tm, NA_WIDTH), lambda i: (i, 0)),
            pl.BlockSpec((tm, SSM_WIDTH), lambda i: (i, 0)),
            pl.BlockSpec((NA_WIDTH, d), lambda i: (0, 0), pipeline_mode=resident),
            pl.BlockSpec((SSM_WIDTH, d), lambda i: (1, 0), pipeline_mode=resident),
            pl.BlockSpec((tm, d), lambda i: (i, 0)),
            pl.BlockSpec((1, 1, d), mod_map),
        ],
        out_specs=pl.BlockSpec((tm, d), lambda i: (i, 0)),
        out_shape=jax.ShapeDtypeStruct((m, d), F32),
        compiler_params=_params("arbitrary"),
        name="outproj",
    )(attn, ssm, w_out, w_out, x2d, gate)


def _shift_matrix():
    s = np.zeros((D_CONV * CHUNK, EXT_ROWS), np.float32)
    r = np.arange(CHUNK)
    for t in range(D_CONV):
        s[t * CHUNK + r, HALO + r + t - D_CONV // 2] = 1.0
    return jnp.asarray(s, BF16)


def _rope_tables(n_tokens):
    t = jnp.arange(n_tokens)
    rows = (t // GRID_W).astype(F32)
    cols = (t % GRID_W).astype(F32)
    n_pairs = HEAD_DIM // 4
    freqs = ROPE_THETA ** (-jnp.arange(n_pairs, dtype=F32) / n_pairs)
    ang = jnp.concatenate([rows[:, None] * freqs, cols[:, None] * freqs], axis=-1)
    cos, sin = jnp.cos(ang), jnp.sin(ang)
    cos_rep = jnp.repeat(cos, 2, axis=-1)
    sin_signed = jnp.stack([-sin, sin], axis=-1).reshape(n_tokens, HEAD_DIM)
    return cos_rep, sin_signed


def kernel(x, c, ctx, c_ctx, ada_w, ada_b, norm_w, w_in, q_norm, k_norm, rpb, conv_w, conv_b, dt_bias,
           a_log, d_skip, ssm_norm, w_out):
    bsz, seq_len, d = x.shape
    ctx_len = ctx.shape[1]
    depth = ada_w.shape[0]
    assert d == D_MODEL and seq_len % 1024 == 0 and seq_len >= K_BLK + Q_BLK and ctx_len % CHUNK == 0

    cond = jnp.zeros((8, d), F32).at[:bsz].set(c).at[bsz].set(c_ctx)
    mod = _modulation(cond, ada_w, ada_b)

    cos_x, sin_x = _rope_tables(seq_len)
    cos_c = jnp.ones((ctx_len, HEAD_DIM), F32)
    sin_c = jnp.zeros((ctx_len, HEAD_DIM), F32)

    w_main = w_in[:, :, :MAIN_COLS].astype(BF16)
    w_dt = jnp.pad(w_in[:, :, MAIN_COLS:], ((0, 0), (0, 0), (0, DT_PAD - 2 * SSM_HEADS))).astype(BF16)
    w_out_b = w_out.astype(BF16)
    pad_dt = lambda u, v: jnp.pad(u.reshape(1, 2 * SSM_HEADS), ((0, 0), (0, DT_PAD - 2 * SSM_HEADS)),
                                  constant_values=v)

    shift_mat = _shift_matrix()
    xs2 = x.reshape(bsz * seq_len, d)
    xc2 = ctx.reshape(bsz * ctx_len, d)
    tm_x = 1024
    tm_c = ctx_len

    for l in range(depth):
        update_ctx = l < depth - 1
        shift, scale, gate = mod[l, :, :d], mod[l, :, d:2 * d], mod[l, :, 2 * d:]
        sc_x, sh_x, gt_x = (1.0 + scale[:bsz])[:, None], shift[:bsz][:, None], gate[:bsz][:, None]
        sc_c, sh_c, gt_c = ((1.0 + scale[bsz:bsz + 1])[:, None], shift[bsz:bsz + 1][:, None],
                            gate[bsz:bsz + 1][:, None])
        nw = norm_w[l].reshape(1, d)
        qn = q_norm[l].reshape(1, HEAD_DIM) * (HEAD_DIM ** -0.5 * LOG2E)
        kn = k_norm[l].reshape(1, HEAD_DIM)
        cw = jnp.pad(conv_w[l], ((0, 8 - D_CONV), (0, 0)))
        consts = {
            "shift": shift_mat,
            "conv_w_x": cw[:, :SSM_WIDTH], "conv_w_bc": cw[:, SSM_WIDTH:],
            "conv_b_x": conv_b[l][None, :SSM_WIDTH], "conv_b_bc": conv_b[l][None, SSM_WIDTH:],
            "dt_bias": pad_dt(dt_bias[l], 0.0), "a": pad_dt(-jnp.exp(a_log[l].astype(F32)), -1.0),
            "d_skip": jnp.repeat(d_skip[l], SSM_HEAD_DIM)[None, :], "ssm_norm": ssm_norm[l][None, :],
        }

        pc, dtc = _inproj(xc2, sc_c, sh_c, nw, w_main[l], w_dt[l], qn, kn, cos_c, sin_c,
                          tm=tm_c, seq_len=ctx_len)
        p, dtx = _inproj(xs2, sc_x, sh_x, nw, w_main[l], w_dt[l], qn, kn, cos_x, sin_x,
                         tm=tm_x, seq_len=seq_len)

        pc3, dtc3 = pc.reshape(bsz, ctx_len, MAIN_COLS), dtc.reshape(bsz, ctx_len, DT_PAD)
        p3, dtx3 = p.reshape(bsz, seq_len, MAIN_COLS), dtx.reshape(bsz, seq_len, DT_PAD)
        prep_c = _ssd_prep(pc3, dtc3, consts, bsz=bsz, seq_len=ctx_len)
        prep_x = _ssd_prep(p3, dtx3, consts, bsz=bsz, seq_len=seq_len)
        ycb, s_b = _ssd_pass(pc3, prep_c, consts, None, None, bsz=bsz, seq_len=ctx_len, reverse=True)
        ssm_c, s_f = _ssd_pass(pc3, prep_c, consts, None, ycb, bsz=bsz, seq_len=ctx_len, reverse=False)
        yb, _ = _ssd_pass(p3, prep_x, consts, s_b, None, bsz=bsz, seq_len=seq_len, reverse=True)
        ssm_x, _ = _ssd_pass(p3, prep_x, consts, s_f, yb, bsz=bsz, seq_len=seq_len, reverse=False)
        ssm_c = ssm_c.reshape(bsz * ctx_len, SSM_WIDTH)
        ssm_x = ssm_x.reshape(bsz * seq_len, SSM_WIDTH)

        bias = _attn_bias_tables(rpb[l], seq_len // GRID_W)
        attn_x = _nbr_attention(p, pc, bias, bsz=bsz, seq_len=seq_len, ctx_len=ctx_len)
        xs2 = _outproj(attn_x, ssm_x, w_out_b[l], xs2, gt_x, tm=512, seq_len=seq_len)
        if update_ctx:
            attn_c = _ctx_attention(pc, bsz=bsz, ctx_len=ctx_len)
            xc2 = _outproj(attn_c, ssm_c, w_out_b[l], xc2, gt_c, tm=ctx_len, seq_len=ctx_len)

    return xs2.reshape(bsz, seq_len, d)
```

```python
import functools

import jax
import jax.numpy as jnp
import numpy as np
from jax import lax
from jax.experimental import pallas as pl
from jax.experimental.pallas import tpu as pltpu

F32 = jnp.float32
BF16 = jnp.bfloat16

D_MODEL = 2048
GRID_W = 64
EPS = 1e-6
HEAD_DIM = 128
NA_HEADS = D_MODEL // HEAD_DIM
NA_WIDTH = NA_HEADS * HEAD_DIM
WIN_H = 8
WIN_W = 16
ROPE_THETA = 10000.0
SSM_WIDTH = D_MODEL
SSM_HEAD_DIM = 64
SSM_HEADS = SSM_WIDTH // SSM_HEAD_DIM
SSM_STATE = 128
SSM_GROUPS = 8
HEADS_PER_GROUP = SSM_HEADS // SSM_GROUPS
D_CONV = 5
CHUNK = 128
BC_WIDTH = 2 * SSM_GROUPS * SSM_STATE
MAIN_COLS = 4 * NA_WIDTH + SSM_WIDTH + SSM_WIDTH + BC_WIDTH
DT_PAD = 128
LANES = 128
HALO = 16
SSD_SEQS_PER_STEP = 2
N_SSD_FIXED_INPUTS = 7
EXT_ROWS = 256
LOG2E = 1.4426950408889634
Q_ROWS = 4
Q_BLK = Q_ROWS * GRID_W
K_ROWS = Q_ROWS + WIN_H
K_BLK = K_ROWS * GRID_W
NEG = -1e30
VMEM_LIMIT = 56 * 1024 * 1024

COL_Q, COL_K, COL_V, COL_G, COL_Z, COL_X, COL_BC = 0, 1, 2, 3, 4, 5, 6


def _silu(u):
    h = 0.5 * u
    return h + h * jnp.tanh(h)


def _params(*sem):
    return pltpu.CompilerParams(dimension_semantics=sem, vmem_limit_bytes=VMEM_LIMIT)


def _mod_kernel(c_ref, w_ref, b_ref, o_ref):
    s = _silu(c_ref[...])
    o_ref[0] = jnp.dot(s, w_ref[0], preferred_element_type=F32) + b_ref[0]


def _modulation(cond, ada_w, ada_b):
    depth, d, n = ada_w.shape
    rows = cond.shape[0]
    tn = 512
    return pl.pallas_call(
        _mod_kernel,
        grid=(depth, n // tn),
        in_specs=[
            pl.BlockSpec((rows, d), lambda l, j: (0, 0)),
            pl.BlockSpec((1, d, tn), lambda l, j: (l, 0, j)),
            pl.BlockSpec((1, 1, tn), lambda l, j: (l, 0, j)),
        ],
        out_specs=pl.BlockSpec((1, rows, tn), lambda l, j: (l, 0, j)),
        out_shape=jax.ShapeDtypeStruct((depth, rows, n), F32),
        compiler_params=_params("arbitrary", "arbitrary"),
        name="modulation",
    )(cond, ada_w, ada_b.reshape(depth, 1, n))


def _inproj_kernel(x_ref, sc_ref, sh_ref, nw_ref, w_ref, wdt_ref, qn_ref, kn_ref, cos_ref, sin_ref,
                   o_ref, dt_ref, h_ref, *, qk_blocks, heads_per_block):
    j = pl.program_id(1)

    @pl.when(j == 0)
    def _():
        xf = x_ref[...]
        ms = jnp.mean(xf * xf, axis=-1, keepdims=True)
        y = xf * lax.rsqrt(ms + EPS) * nw_ref[...]
        hb = (y * sc_ref[0] + sh_ref[0]).astype(BF16)
        h_ref[...] = hb
        dt_ref[...] = jnp.dot(hb, wdt_ref[...], preferred_element_type=F32)

    acc = jnp.dot(h_ref[...], w_ref[...], preferred_element_type=F32)

    @pl.when(j < 2 * qk_blocks)
    def _():
        nrm = jnp.where(j < qk_blocks, qn_ref[...], kn_ref[...])
        cos = cos_ref[...]
        sin = sin_ref[...]
        even = (lax.broadcasted_iota(jnp.int32, cos.shape, 1) % 2) == 0
        for hh in range(heads_per_block):
            u = acc[:, hh * HEAD_DIM:(hh + 1) * HEAD_DIM]
            inv = lax.rsqrt(jnp.mean(u * u, axis=-1, keepdims=True) + EPS)
            u = u * nrm
            partner = jnp.where(even, pltpu.roll(u, HEAD_DIM - 1, 1), pltpu.roll(u, 1, 1))
            o_ref[:, hh * HEAD_DIM:(hh + 1) * HEAD_DIM] = ((u * cos + partner * sin) * inv).astype(o_ref.dtype)

    @pl.when(j >= 2 * qk_blocks)
    def _():
        o_ref[...] = acc.astype(o_ref.dtype)


def _inproj(x2d, scale1, shift, norm_w, w_main, w_dt, q_norm, k_norm, cos_t, sin_t, *, tm, seq_len):
    m, d = x2d.shape
    n = w_main.shape[1]
    tn = 1024
    blocks_per_seq = seq_len // tm
    tab_blocks = cos_t.shape[0] // tm
    per_batch = scale1.shape[0] > 1
    mod_map = (lambda i, j: (i // blocks_per_seq, 0, 0)) if per_batch else (lambda i, j: (0, 0, 0))
    kern = functools.partial(_inproj_kernel, qk_blocks=NA_WIDTH // tn, heads_per_block=tn // HEAD_DIM)
    return pl.pallas_call(
        kern,
        grid=(m // tm, n // tn),
        in_specs=[
            pl.BlockSpec((tm, d), lambda i, j: (i, 0)),
            pl.BlockSpec((1, 1, d), mod_map),
            pl.BlockSpec((1, 1, d), mod_map),
            pl.BlockSpec((1, d), lambda i, j: (0, 0)),
            pl.BlockSpec((d, tn), lambda i, j: (0, j)),
            pl.BlockSpec((d, DT_PAD), lambda i, j: (0, 0)),
            pl.BlockSpec((1, HEAD_DIM), lambda i, j: (0, 0)),
            pl.BlockSpec((1, HEAD_DIM), lambda i, j: (0, 0)),
            pl.BlockSpec((tm, HEAD_DIM), lambda i, j: (i % tab_blocks, 0)),
            pl.BlockSpec((tm, HEAD_DIM), lambda i, j: (i % tab_blocks, 0)),
        ],
        out_specs=[
            pl.BlockSpec((tm, tn), lambda i, j: (i, j)),
            pl.BlockSpec((tm, DT_PAD), lambda i, j: (i, 0)),
        ],
        out_shape=[
            jax.ShapeDtypeStruct((m, n), BF16),
            jax.ShapeDtypeStruct((m, DT_PAD), F32),
        ],
        scratch_shapes=[pltpu.VMEM((tm, d), BF16)],
        compiler_params=_params("arbitrary", "arbitrary"),
        name="inproj",
    )(x2d, scale1, shift, norm_w, w_main, w_dt, q_norm, k_norm, cos_t, sin_t)


def _nbr_attn_kernel(q_ref, k_ref, v_ref, g_ref, kc_ref, vc_ref, bias_ref, o_ref, sl0, sc0, sl1, sc1,
                     *, n_qblk, seq_len):
    kc = kc_ref[...]
    vc = vc_ref[...]
    nt = (((1,), (1,)), ((), ()))

    def key_start(jq):
        return pl.multiple_of(jnp.clip(jq * Q_BLK - (WIN_H // 2) * GRID_W, 0, seq_len - K_BLK), GRID_W)

    def scores(jq, sl_ref, sc_ref):
        variant = jnp.where(jq == 0, 0, jnp.where(jq == n_qblk - 1, 2, 1))
        q = q_ref[pl.ds(pl.multiple_of(jq * Q_BLK, Q_BLK), Q_BLK), :]
        kw = k_ref[pl.ds(key_start(jq), K_BLK), :]
        sl_ref[...] = lax.dot_general(q, kw, nt, preferred_element_type=F32) + bias_ref[variant, 0]
        sc_ref[...] = lax.dot_general(q, kc, nt, preferred_element_type=F32)

    def finish(jq, sl_ref, sc_ref):
        q0 = pl.multiple_of(jq * Q_BLK, Q_BLK)
        vw = v_ref[pl.ds(key_start(jq), K_BLK), :]
        s_loc = sl_ref[...]
        s_ctx = sc_ref[...]
        mx = jnp.maximum(jnp.max(s_loc, axis=-1, keepdims=True), jnp.max(s_ctx, axis=-1, keepdims=True))
        p_loc = jnp.exp2(s_loc - mx)
        p_ctx = jnp.exp2(s_ctx - mx)
        den = jnp.sum(p_loc, axis=-1, keepdims=True) + jnp.sum(p_ctx, axis=-1, keepdims=True)
        o = (jnp.dot(p_loc.astype(BF16), vw, preferred_element_type=F32)
             + jnp.dot(p_ctx.astype(BF16), vc, preferred_element_type=F32))
        g = g_ref[pl.ds(q0, Q_BLK), :].astype(F32)
        o_ref[pl.ds(q0, Q_BLK), :] = (o * (1.0 / den) * _silu(g)).astype(o_ref.dtype)

    scores(0, sl0, sc0)

    def body(i, carry):
        scores(2 * i + 1, sl1, sc1)
        finish(2 * i, sl0, sc0)
        scores(jnp.minimum(2 * i + 2, n_qblk - 1), sl0, sc0)
        finish(2 * i + 1, sl1, sc1)
        return carry

    lax.fori_loop(0, n_qblk // 2, body, 0)


def _nbr_attention(p, pc, bias, *, bsz, seq_len, ctx_len):
    blk = NA_WIDTH // HEAD_DIM
    n_qblk = seq_len // Q_BLK
    kern = functools.partial(_nbr_attn_kernel, n_qblk=n_qblk, seq_len=seq_len)
    sec = lambda c: (lambda b, h: (b, c * blk + h))
    return pl.pallas_call(
        kern,
        grid=(bsz, NA_HEADS),
        in_specs=[
            pl.BlockSpec((seq_len, HEAD_DIM), sec(COL_Q)),
            pl.BlockSpec((seq_len, HEAD_DIM), sec(COL_K)),
            pl.BlockSpec((seq_len, HEAD_DIM), sec(COL_V)),
            pl.BlockSpec((seq_len, HEAD_DIM), sec(COL_G)),
            pl.BlockSpec((ctx_len, HEAD_DIM), sec(COL_K)),
            pl.BlockSpec((ctx_len, HEAD_DIM), sec(COL_V)),
            pl.BlockSpec((3, 1, Q_BLK, K_BLK), lambda b, h: (0, h, 0, 0)),
        ],
        out_specs=pl.BlockSpec((seq_len, HEAD_DIM), lambda b, h: (b, h)),
        out_shape=jax.ShapeDtypeStruct((bsz * seq_len, NA_WIDTH), BF16),
        scratch_shapes=[pltpu.VMEM((Q_BLK, K_BLK), F32), pltpu.VMEM((Q_BLK, ctx_len), F32),
                        pltpu.VMEM((Q_BLK, K_BLK), F32), pltpu.VMEM((Q_BLK, ctx_len), F32)],
        compiler_params=_params("arbitrary", "arbitrary"),
        name="nbr_attention",
    )(p, p, p, p, pc, pc, bias)


def _attn_bias_tables(rpb, n_rows):
    qi = np.arange(Q_BLK)
    ki = np.arange(K_BLK)
    qr, qc = qi // GRID_W, qi % GRID_W
    kr, kcol = ki // GRID_W, ki % GRID_W
    drow = kr[None, :] - qr[:, None]
    c0 = np.clip(qc - WIN_W // 2, 0, GRID_W - WIN_W)
    col_ok = (kcol[None, :] >= c0[:, None]) & (kcol[None, :] < c0[:, None] + WIN_W)
    col_idx = np.clip(kcol[None, :] - qc[:, None] + WIN_W - 1, 0, 2 * WIN_W - 2)
    half = WIN_H // 2
    row_ok = [
        (kr[None, :] < WIN_H) & np.ones_like(drow, bool),
        (drow >= 0) & (drow < WIN_H),
        (kr[None, :] >= K_ROWS - WIN_H) & np.ones_like(drow, bool),
    ]
    row_idx = [drow + WIN_H - 1, drow - half + WIN_H - 1, drow - (K_ROWS - Q_ROWS) + WIN_H - 1]
    n_r, n_c = 2 * WIN_H - 1, 2 * WIN_W - 1
    col_oh = np.zeros((GRID_W, GRID_W, n_c), np.float32)
    col_oh[np.arange(GRID_W)[:, None], np.arange(GRID_W)[None, :], col_idx[:GRID_W, :GRID_W]] = 1.0
    row_oh = np.zeros((3, Q_ROWS, K_ROWS, n_r), np.float32)
    for v, ridx in enumerate(row_idx):
        r = np.clip(ridx[::GRID_W, ::GRID_W], 0, n_r - 1)
        row_oh[v, np.arange(Q_ROWS)[:, None], np.arange(K_ROWS)[None, :], r] = 1.0
    vals = jnp.einsum("vqki,hij,cdj->vhqckd", jnp.asarray(row_oh), rpb.astype(F32) * LOG2E,
                      jnp.asarray(col_oh), precision=lax.Precision.HIGHEST)
    vals = vals.reshape(3, NA_HEADS, Q_BLK, K_BLK)
    ok = np.stack([r & col_ok for r in row_ok], axis=0)[:, None]
    return jnp.where(jnp.asarray(ok), vals, NEG)


def _ctx_attn_kernel(q_ref, k_ref, v_ref, g_ref, o_ref):
    s = lax.dot_general(q_ref[...], k_ref[...], (((1,), (1,)), ((), ())), preferred_element_type=F32)
    p = jnp.exp2(s - jnp.max(s, axis=-1, keepdims=True))
    den = jnp.sum(p, axis=-1, keepdims=True)
    o = jnp.dot(p.astype(BF16), v_ref[...], preferred_element_type=F32)
    o_ref[...] = (o * (1.0 / den) * _silu(g_ref[...].astype(F32))).astype(o_ref.dtype)


def _ctx_attention(pc, *, bsz, ctx_len):
    blk = NA_WIDTH // HEAD_DIM
    sec = lambda c: (lambda b, h: (b, c * blk + h))
    return pl.pallas_call(
        _ctx_attn_kernel,
        grid=(bsz, NA_HEADS),
        in_specs=[pl.BlockSpec((ctx_len, HEAD_DIM), sec(c)) for c in (COL_Q, COL_K, COL_V, COL_G)],
        out_specs=pl.BlockSpec((ctx_len, HEAD_DIM), lambda b, h: (b, h)),
        out_shape=jax.ShapeDtypeStruct((bsz * ctx_len, NA_WIDTH), BF16),
        compiler_params=_params("arbitrary", "arbitrary"),
        name="ctx_attention",
    )(pc, pc, pc, pc)


def _split3(u):
    hi = u.astype(BF16)
    r1 = u - hi.astype(F32)
    mid = r1.astype(BF16)
    lo = (r1 - mid.astype(F32)).astype(BF16)
    return hi, mid, lo


def _ssd_prep_kernel(xs_ref, bc_ref, xs_prev, bc_prev, xs_next, bc_next, dt_ref, cwx_ref, cwb_ref, cbx_ref,
                     cbb_ref, dtb_ref, a_ref, shift_ref,
                     ax_ref, ac_ref, abt_ref, cs_ref, ecs_ref, wgt_ref, cslt_ref, *, n_chunks):
    cc = pl.program_id(1)

    raw = dt_ref[0] + dtb_ref[...]
    dt = jnp.maximum(raw, 0.0) + jnp.log1p(jnp.exp(-jnp.abs(raw)))
    da = dt * a_ref[...]
    row = lax.broadcasted_iota(jnp.int32, (CHUNK, CHUNK), 0)
    col = lax.broadcasted_iota(jnp.int32, (CHUNK, CHUNK), 1)
    tri_f = jnp.where(col <= row, 1.0, 0.0).astype(BF16)
    tri_b = jnp.where(col >= row, 1.0, 0.0).astype(BF16)
    parts = _split3(da)
    cs_f = sum(jnp.dot(tri_f, u, preferred_element_type=F32) for u in parts)
    cs_b = sum(jnp.dot(tri_b, u, preferred_element_type=F32) for u in parts)
    fwd_lane = lax.broadcasted_iota(jnp.int32, (CHUNK, DT_PAD), 1) < SSM_HEADS
    cs = jnp.where(fwd_lane, cs_f, cs_b)
    tot = jnp.where(fwd_lane[0:1], cs[CHUNK - 1:CHUNK, :], cs[0:1, :])
    cs_ref[0] = cs
    ecs_ref[0] = jnp.exp(cs)
    wgt_ref[0] = jnp.exp(tot - cs) * dt
    cslt_ref[0] = (cs - jnp.log(dt)).T

    def conv(cur_ref, prev_ref, next_ref, w_ref, b_ref):
        width = cur_ref.shape[-1]
        zero_halo = jnp.zeros((HALO, width), BF16)
        ext = jnp.concatenate([
            jnp.where(cc > 0, prev_ref[0], zero_halo), cur_ref[0],
            jnp.where(cc < n_chunks - 1, next_ref[0], zero_halo),
            jnp.zeros((EXT_ROWS - 2 * HALO - CHUNK, width), BF16)], axis=0)
        taps = jnp.dot(shift_ref[...], ext, preferred_element_type=F32)
        mid = D_CONV // 2
        acc = b_ref[...] + cur_ref[0].astype(F32) * w_ref[mid:mid + 1, :]
        for i, t in enumerate(t for t in range(D_CONV) if t != mid):
            acc = acc + taps[i * CHUNK:(i + 1) * CHUNK] * w_ref[t:t + 1, :]
        return _silu(acc)

    ax_ref[0] = conv(xs_ref, xs_prev, xs_next, cwx_ref, cbx_ref).astype(BF16)
    bcm = conv(bc_ref, bc_prev, bc_next, cwb_ref, cbb_ref)
    half = SSM_GROUPS * SSM_STATE
    ac_ref[0] = bcm[:, half:].astype(BF16)
    abt_ref[0] = bcm[:, :half].T.astype(BF16)


def _ssd_prep(p, dt_raw, consts, *, bsz, seq_len):
    n_chunks = seq_len // CHUNK
    wide = NA_WIDTH
    per_chunk = CHUNK // HALO
    last_halo = seq_len // HALO - 1
    half = SSM_GROUPS * SSM_STATE
    cur = lambda col: (lambda b, c: (b, c, col))
    prev = lambda col: (lambda b, c: (b, jnp.maximum(c * per_chunk - 1, 0), col))
    nxt = lambda col: (lambda b, c: (b, jnp.minimum(c * per_chunk + per_chunk, last_halo), col))
    const2 = lambda b, c: (0, 0)
    tab_spec = pl.BlockSpec((1, CHUNK, DT_PAD), cur(0))
    tab_shape = jax.ShapeDtypeStruct((bsz, seq_len, DT_PAD), F32)
    return pl.pallas_call(
        functools.partial(_ssd_prep_kernel, n_chunks=n_chunks),
        grid=(bsz, n_chunks),
        in_specs=[
            pl.BlockSpec((1, CHUNK, wide), cur(COL_X)), pl.BlockSpec((1, CHUNK, wide), cur(COL_BC)),
            pl.BlockSpec((1, HALO, wide), prev(COL_X)), pl.BlockSpec((1, HALO, wide), prev(COL_BC)),
            pl.BlockSpec((1, HALO, wide), nxt(COL_X)), pl.BlockSpec((1, HALO, wide), nxt(COL_BC)),
            pl.BlockSpec((1, CHUNK, DT_PAD), cur(0)),
            pl.BlockSpec((8, SSM_WIDTH), const2), pl.BlockSpec((8, BC_WIDTH), const2),
            pl.BlockSpec((1, SSM_WIDTH), const2), pl.BlockSpec((1, BC_WIDTH), const2),
            pl.BlockSpec((1, DT_PAD), const2), pl.BlockSpec((1, DT_PAD), const2),
            pl.BlockSpec(((D_CONV - 1) * CHUNK, EXT_ROWS), const2),
        ],
        out_specs=[
            pl.BlockSpec((1, CHUNK, SSM_WIDTH), cur(0)),
            pl.BlockSpec((1, CHUNK, half), cur(0)),
            pl.BlockSpec((1, half, CHUNK), cur(0)),
            tab_spec, tab_spec, tab_spec, tab_spec,
        ],
        out_shape=[
            jax.ShapeDtypeStruct((bsz, seq_len, SSM_WIDTH), BF16),
            jax.ShapeDtypeStruct((bsz, seq_len, half), BF16),
            jax.ShapeDtypeStruct((bsz, n_chunks * half, CHUNK), BF16),
            tab_shape, tab_shape, tab_shape, tab_shape,
        ],
        compiler_params=_params("arbitrary", "arbitrary"),
        name="ssd_prep",
    )(p, p, p, p, p, p, dt_raw, consts["conv_w_x"], consts["conv_w_bc"], consts["conv_b_x"],
      consts["conv_b_bc"], consts["dt_bias"], consts["a"], consts["shift"])


def _ssd_kernel(*refs, n_seq, reverse, has_init, finalize):
    st_ref = refs[-1]
    init_ref = refs[N_SSD_FIXED_INPUTS] if has_init else None

    @pl.when(pl.program_id(1) == 0)
    def _():
        if has_init:
            st_ref[...] = init_ref[...]
        else:
            st_ref[...] = jnp.zeros(st_ref.shape, F32)

    for bi in range(n_seq):
        _ssd_chunk(bi, *refs, reverse=reverse, has_init=has_init, finalize=finalize)


def _ssd_chunk(bi, *refs, reverse, has_init, finalize):
    it = iter(refs)
    ax_ref, ac_ref, abt_ref, cs_ref, ecs_ref, wgt_ref, cslt_ref = (next(it) for _ in range(N_SSD_FIXED_INPUTS))
    init_ref = next(it) if has_init else None
    if finalize:
        z_ref, yb_ref, dskip_ref, nw_ref = next(it), next(it), next(it), next(it)
    y_ref, st_ref = next(it), next(it)

    xs_b = ax_ref[bi]
    xs = xs_b.astype(F32)
    cs = cs_ref[bi]
    ecs = ecs_ref[bi]
    wgt = wgt_ref[bi]
    csl_t = cslt_ref[bi]
    etot = ecs[0:1, :] if reverse else ecs[CHUNK - 1:CHUNK, :]
    row = lax.broadcasted_iota(jnp.int32, (CHUNK, CHUNK), 0)
    col = lax.broadcasted_iota(jnp.int32, (CHUNK, CHUNK), 1)
    causal = (col >= row) if reverse else (col <= row)
    lane = lax.broadcasted_iota(jnp.int32, (CHUNK, LANES), 1)
    low_half = lane < SSM_HEAD_DIM
    lane_row = lax.broadcasted_iota(jnp.int32, (1, LANES), 1) < SSM_HEAD_DIM
    off = SSM_HEADS if reverse else 0

    y_parts = []
    for g in range(SSM_GROUPS):
        c_g = ac_ref[bi, :, g * SSM_STATE:(g + 1) * SSM_STATE]
        b_gt = abt_ref[bi, g * SSM_STATE:(g + 1) * SSM_STATE, :]
        cb = jnp.dot(c_g, b_gt, preferred_element_type=F32)
        gw = HEADS_PER_GROUP * SSM_HEAD_DIM
        st_g = st_ref[bi, :, g * gw:(g + 1) * gw]
        y_off = jnp.dot(c_g, st_g.astype(BF16), preferred_element_type=F32)
        new_cols = []
        for pr in range(HEADS_PER_GROUP // 2):
            h0 = g * HEADS_PER_GROUP + 2 * pr
            lanes0 = h0 * SSM_HEAD_DIM
            m_pair = []
            for h in (h0, h0 + 1):
                hc = off + h
                seg = cs[:, hc:hc + 1] - csl_t[hc:hc + 1, :]
                decay = jnp.exp(jnp.where(causal, seg, NEG))
                m_pair.append((cb * decay).astype(BF16))
            m_cat = jnp.concatenate(m_pair, axis=1)
            x_pair = xs_b[:, lanes0:lanes0 + LANES]
            zero = jnp.zeros_like(x_pair)
            x_diag = jnp.concatenate([jnp.where(low_half, x_pair, zero), jnp.where(low_half, zero, x_pair)], axis=0)
            y_diag = jnp.dot(m_cat, x_diag, preferred_element_type=F32)
            e_pair = jnp.where(low_half, ecs[:, off + h0:off + h0 + 1], ecs[:, off + h0 + 1:off + h0 + 2])
            w_pair = jnp.where(low_half, wgt[:, off + h0:off + h0 + 1], wgt[:, off + h0 + 1:off + h0 + 2])
            t_pair = jnp.where(lane_row, etot[:, off + h0:off + h0 + 1], etot[:, off + h0 + 1:off + h0 + 2])
            y_parts.append(y_diag + y_off[:, 2 * pr * SSM_HEAD_DIM:2 * pr * SSM_HEAD_DIM + LANES] * e_pair)
            xw = (xs[:, lanes0:lanes0 + LANES] * w_pair).astype(BF16)
            upd = jnp.dot(b_gt, xw, preferred_element_type=F32)
            new_cols.append(st_g[:, 2 * pr * SSM_HEAD_DIM:2 * pr * SSM_HEAD_DIM + LANES] * t_pair + upd)
        st_ref[bi, :, g * gw:(g + 1) * gw] = jnp.concatenate(new_cols, axis=1)
    y = jnp.concatenate(y_parts, axis=1)

    if finalize:
        y = y + yb_ref[bi].astype(F32) + xs * dskip_ref[...]
        y = y * _silu(z_ref[bi].astype(F32))
        ms = jnp.mean(y * y, axis=-1, keepdims=True)
        y = y * lax.rsqrt(ms + EPS) * nw_ref[...]
    y_ref[bi] = y.astype(y_ref.dtype)


def _ssd_pass(p, prep, consts, init_state, y_bwd, *, bsz, seq_len, reverse):
    n_chunks = seq_len // CHUNK
    has_init = init_state is not None
    finalize = y_bwd is not None
    n_seq = SSD_SEQS_PER_STEP
    assert bsz % n_seq == 0
    wide = NA_WIDTH
    half = SSM_GROUPS * SSM_STATE

    def chunk_of(c):
        return (n_chunks - 1 - c) if reverse else c

    cur = lambda col: (lambda b, c: (b, chunk_of(c), col))
    const2 = lambda b, c: (0, 0)
    state_spec = pl.BlockSpec((n_seq, SSM_STATE, SSM_WIDTH), lambda b, c: (b, 0, 0))
    tab_spec = pl.BlockSpec((n_seq, CHUNK, DT_PAD), cur(0))

    in_specs = [
        pl.BlockSpec((n_seq, CHUNK, SSM_WIDTH), cur(0)),
        pl.BlockSpec((n_seq, CHUNK, half), cur(0)),
        pl.BlockSpec((n_seq, half, CHUNK), cur(0)),
        tab_spec, tab_spec, tab_spec, tab_spec,
    ]
    args = list(prep)
    assert len(args) == N_SSD_FIXED_INPUTS
    if has_init:
        in_specs.append(state_spec)
        args.append(init_state)
    if finalize:
        in_specs += [
            pl.BlockSpec((n_seq, CHUNK, wide), cur(COL_Z)),
            pl.BlockSpec((n_seq, CHUNK, SSM_WIDTH), cur(0)),
            pl.BlockSpec((1, SSM_WIDTH), const2), pl.BlockSpec((1, SSM_WIDTH), const2),
        ]
        args += [p, y_bwd, consts["d_skip"], consts["ssm_norm"]]
    kern = functools.partial(_ssd_kernel, n_seq=n_seq, reverse=reverse, has_init=has_init, finalize=finalize)
    return pl.pallas_call(
        kern,
        grid=(bsz // n_seq, n_chunks),
        in_specs=in_specs,
        out_specs=[pl.BlockSpec((n_seq, CHUNK, SSM_WIDTH), cur(0)), state_spec],
        out_shape=[
            jax.ShapeDtypeStruct((bsz, seq_len, SSM_WIDTH), BF16),
            jax.ShapeDtypeStruct((bsz, SSM_STATE, SSM_WIDTH), F32),
        ],
        compiler_params=_params("arbitrary", "arbitrary"),
        name="ssd_bwd" if reverse else "ssd_fwd",
    )(*args)


def _outproj_kernel(a_ref, s_ref, wa_ref, ws_ref, x_ref, g_ref, o_ref):
    acc = (jnp.dot(a_ref[...], wa_ref[...], preferred_element_type=F32)
           + jnp.dot(s_ref[0], ws_ref[...], preferred_element_type=F32))
    o_ref[...] = x_ref[...] + g_ref[0] * acc


def _outproj(attn, ssm, w_out, x2d, gate, *, tm, seq_len):
    m, d = x2d.shape
    blocks_per_seq = seq_len // tm
    per_batch = gate.shape[0] > 1
    mod_map = (lambda i: (i // blocks_per_seq, 0, 0)) if per_batch else (lambda i: (0, 0, 0))
    resident = pl.Buffered(1)
    return pl.pallas_call(
        _outproj_kernel,
        grid=(m // tm,),
        in_specs=[
            pl.BlockSpec((tm, NA_WIDTH), lambda i: (i, 0)),
            pl.BlockSpec((1, tm, SSM_WIDTH), lambda i: (i // blocks_per_seq, i % blocks_per_seq, 0)),
            pl.BlockSpec((NA_WIDTH, d), lambda i: (0, 0), pipeline_mode=resident),
            pl.BlockSpec((SSM_WIDTH, d), lambda i: (1, 0), pipeline_mode=resident),
            pl.BlockSpec((tm, d), lambda i: (i, 0)),
            pl.BlockSpec((1, 1, d), mod_map),
        ],
        out_specs=pl.BlockSpec((tm, d), lambda i: (i, 0)),
        out_shape=jax.ShapeDtypeStruct((m, d), F32),
        compiler_params=_params("arbitrary"),
        name="outproj",
    )(attn, ssm, w_out, w_out, x2d, gate)


def _shift_matrix():
    s = np.zeros(((D_CONV - 1) * CHUNK, EXT_ROWS), np.float32)
    r = np.arange(CHUNK)
    for i, t in enumerate(t for t in range(D_CONV) if t != D_CONV // 2):
        s[i * CHUNK + r, HALO + r + t - D_CONV // 2] = 1.0
    return jnp.asarray(s, BF16)


def _rope_tables(n_tokens):
    t = jnp.arange(n_tokens)
    rows = (t // GRID_W).astype(F32)
    cols = (t % GRID_W).astype(F32)
    n_pairs = HEAD_DIM // 4
    freqs = ROPE_THETA ** (-jnp.arange(n_pairs, dtype=F32) / n_pairs)
    ang = jnp.concatenate([rows[:, None] * freqs, cols[:, None] * freqs], axis=-1)
    cos, sin = jnp.cos(ang), jnp.sin(ang)
    cos_rep = jnp.repeat(cos, 2, axis=-1)
    sin_signed = jnp.stack([-sin, sin], axis=-1).reshape(n_tokens, HEAD_DIM)
    return cos_rep, sin_signed


def kernel(x, c, ctx, c_ctx, ada_w, ada_b, norm_w, w_in, q_norm, k_norm, rpb, conv_w, conv_b, dt_bias,
           a_log, d_skip, ssm_norm, w_out):
    bsz, seq_len, d = x.shape
    ctx_len = ctx.shape[1]
    depth = ada_w.shape[0]
    assert d == D_MODEL and seq_len % 1024 == 0 and seq_len >= K_BLK + Q_BLK and ctx_len % CHUNK == 0

    cond = jnp.zeros((8, d), F32).at[:bsz].set(c).at[bsz].set(c_ctx)
    mod = _modulation(cond, ada_w, ada_b)

    cos_x, sin_x = _rope_tables(seq_len)
    cos_c = jnp.ones((ctx_len, HEAD_DIM), F32)
    sin_c = jnp.zeros((ctx_len, HEAD_DIM), F32)

    w_main = w_in[:, :, :MAIN_COLS].astype(BF16)
    w_dt = jnp.pad(w_in[:, :, MAIN_COLS:], ((0, 0), (0, 0), (0, DT_PAD - 2 * SSM_HEADS))).astype(BF16)
    w_out_b = w_out.astype(BF16)
    pad_dt = lambda u, v: jnp.pad(u.reshape(1, 2 * SSM_HEADS), ((0, 0), (0, DT_PAD - 2 * SSM_HEADS)),
                                  constant_values=v)

    shift_mat = _shift_matrix()
    xs2 = x.reshape(bsz * seq_len, d)
    xc2 = ctx.reshape(bsz * ctx_len, d)
    tm_x = 1024
    tm_c = ctx_len

    for l in range(depth):
        update_ctx = l < depth - 1
        shift, scale, gate = mod[l, :, :d], mod[l, :, d:2 * d], mod[l, :, 2 * d:]
        sc_x, sh_x, gt_x = (1.0 + scale[:bsz])[:, None], shift[:bsz][:, None], gate[:bsz][:, None]
        sc_c, sh_c, gt_c = ((1.0 + scale[bsz:bsz + 1])[:, None], shift[bsz:bsz + 1][:, None],
                            gate[bsz:bsz + 1][:, None])
        nw = norm_w[l].reshape(1, d)
        qn = q_norm[l].reshape(1, HEAD_DIM) * (HEAD_DIM ** -0.5 * LOG2E)
        kn = k_norm[l].reshape(1, HEAD_DIM)
        cw = jnp.pad(conv_w[l], ((0, 8 - D_CONV), (0, 0)))
        consts = {
            "shift": shift_mat,
            "conv_w_x": cw[:, :SSM_WIDTH], "conv_w_bc": cw[:, SSM_WIDTH:],
            "conv_b_x": conv_b[l][None, :SSM_WIDTH], "conv_b_bc": conv_b[l][None, SSM_WIDTH:],
            "dt_bias": pad_dt(dt_bias[l], 0.0), "a": pad_dt(-jnp.exp(a_log[l].astype(F32)), -1.0),
            "d_skip": jnp.repeat(d_skip[l], SSM_HEAD_DIM)[None, :], "ssm_norm": ssm_norm[l][None, :],
        }

        pc, dtc = _inproj(xc2, sc_c, sh_c, nw, w_main[l], w_dt[l], qn, kn, cos_c, sin_c,
                          tm=tm_c, seq_len=ctx_len)
        p, dtx = _inproj(xs2, sc_x, sh_x, nw, w_main[l], w_dt[l], qn, kn, cos_x, sin_x,
                         tm=tm_x, seq_len=seq_len)

        pc3, dtc3 = pc.reshape(bsz, ctx_len, MAIN_COLS), dtc.reshape(bsz, ctx_len, DT_PAD)
        p3, dtx3 = p.reshape(bsz, seq_len, MAIN_COLS), dtx.reshape(bsz, seq_len, DT_PAD)
        prep_c = _ssd_prep(pc3, dtc3, consts, bsz=bsz, seq_len=ctx_len)
        prep_x = _ssd_prep(p3, dtx3, consts, bsz=bsz, seq_len=seq_len)
        ycb, s_b = _ssd_pass(pc3, prep_c, consts, None, None, bsz=bsz, seq_len=ctx_len, reverse=True)
        ssm_c, s_f = _ssd_pass(pc3, prep_c, consts, None, ycb, bsz=bsz, seq_len=ctx_len, reverse=False)
        yb, _ = _ssd_pass(p3, prep_x, consts, s_b, None, bsz=bsz, seq_len=seq_len, reverse=True)
        ssm_x, _ = _ssd_pass(p3, prep_x, consts, s_f, yb, bsz=bsz, seq_len=seq_len, reverse=False)

        bias = _attn_bias_tables(rpb[l], seq_len // GRID_W)
        attn_x = _nbr_attention(p, pc, bias, bsz=bsz, seq_len=seq_len, ctx_len=ctx_len)
        xs2 = _outproj(attn_x, ssm_x, w_out_b[l], xs2, gt_x, tm=512, seq_len=seq_len)
        if update_ctx:
            attn_c = _ctx_attention(pc, bsz=bsz, ctx_len=ctx_len)
            xc2 = _outproj(attn_c, ssm_c, w_out_b[l], xc2, gt_c, tm=ctx_len, seq_len=ctx_len)

    return xs2.reshape(bsz, seq_len, d)
```
